```python
import jax, jax.numpy as jnp
from jax import lax
import numpy as np

D_MODEL = 1024
BATCH = 32
SEQ = 2048
DEPTH = 2

N_A_LAYERS = DEPTH // 2
N_B_LAYERS = DEPTH - N_A_LAYERS
A_HEADS = 8
A_KEY_DIM = D_MODEL // A_HEADS
A_VAL_DIM = D_MODEL // A_HEADS
A_F_DIM = A_HEADS * A_KEY_DIM
A_I_DIM = A_HEADS * A_VAL_DIM
A_CHUNK = 32
B_HEADS = 8
B_HEAD_DIM = D_MODEL // B_HEADS
MOBA_BLOCK = 256
MOBA_TOPK = 3
Q_BLOCK = 128
ROPE_THETA = 10000.0
NORM_EPS = 1e-6

kernel_name = "hgrn2_moba_yoco_hybrid"

F32 = jnp.float32


def rms_norm(x, g):
    xf = x.astype(F32)
    y = xf * lax.rsqrt(jnp.mean(xf * xf, axis=-1, keepdims=True) + NORM_EPS)
    return (y * g.astype(F32)).astype(x.dtype)


def modulation(c, w, b):
    m = (jax.nn.silu(c) @ w + b)[:, None, :]
    shift, scale, gate = jnp.split(m, 3, axis=-1)
    return shift, scale, gate


def rope_tables(positions):
    inv = ROPE_THETA ** (-jnp.arange(0, B_HEAD_DIM, 2, dtype=F32) / B_HEAD_DIM)
    ang = positions.astype(F32)[..., None] * inv
    return jnp.cos(ang)[:, :, None, :], jnp.sin(ang)[:, :, None, :]


def apply_rope(x, cos, sin):
    xf = x.astype(F32)
    x1, x2 = jnp.split(xf, 2, axis=-1)
    return jnp.concatenate([x1 * cos - x2 * sin, x2 * cos + x1 * sin], axis=-1).astype(x.dtype)


def hgrn2_mixer(u, w_in, w_out, g_norm, lb):
    bsz, T, _ = u.shape
    n_chunks = T // A_CHUNK
    proj = u @ w_in
    q, f, i, g = jnp.split(proj, [A_F_DIM, 2 * A_F_DIM, 2 * A_F_DIM + A_I_DIM], axis=-1)
    q = jax.nn.silu(q.astype(F32))
    fg = lb + (1.0 - lb) * jax.nn.sigmoid(f.astype(F32))
    k = 1.0 - fg
    logf = jnp.log(fg)

    def heads(t, d):
        return t.reshape(bsz, n_chunks, A_CHUNK, A_HEADS, d).transpose(1, 0, 3, 2, 4)

    qc, kc = heads(q, A_KEY_DIM), heads(k, A_KEY_DIM)
    vc = heads(i.astype(F32), A_VAL_DIM)
    bc = jnp.cumsum(heads(logf, A_KEY_DIM), axis=3)
    causal = jnp.tril(jnp.ones((A_CHUNK, A_CHUNK), dtype=bool))

    def step(S, inp):
        q_, k_, v_, b_ = inp
        diff = b_[:, :, :, None, :] - b_[:, :, None, :, :]
        decay = jnp.exp(jnp.where(causal[:, :, None], diff, -jnp.inf))
        att = jnp.einsum('bhtd,bhtsd,bhsd->bhts', q_, decay, k_)
        o = jnp.einsum('bhts,bhsv->bhtv', att, v_) + jnp.einsum('bhtd,bhdv->bhtv', q_ * jnp.exp(b_), S)
        b_last = b_[:, :, -1, :]
        S = jnp.exp(b_last)[..., None] * S + jnp.einsum(
            'bhsd,bhsv->bhdv', k_ * jnp.exp(b_last[:, :, None, :] - b_), v_)
        return S, o

    S0 = jnp.zeros((bsz, A_HEADS, A_KEY_DIM, A_VAL_DIM), F32)
    _, o = lax.scan(step, S0, (qc, kc, vc, bc))
    o = o.transpose(1, 0, 3, 2, 4).reshape(bsz, T, A_HEADS, A_VAL_DIM)
    o = o * lax.rsqrt(jnp.mean(o * o, axis=-1, keepdims=True) + NORM_EPS) * g_norm.astype(F32)
    o = o * jax.nn.silu(g.astype(F32).reshape(bsz, T, A_HEADS, A_VAL_DIM))
    return o.reshape(bsz, T, A_I_DIM).astype(u.dtype) @ w_out


def shared_kv(h, g, w_kv, cos, sin):
    bsz, T, _ = h.shape
    k, v = jnp.split(rms_norm(h, g) @ w_kv, 2, axis=-1)
    k = apply_rope(k.reshape(bsz, T, B_HEADS, B_HEAD_DIM), cos, sin)
    v = v.reshape(bsz, T, B_HEADS, B_HEAD_DIM)
    n_blocks = -(-T // MOBA_BLOCK)
    pad = n_blocks * MOBA_BLOCK - T
    k = jnp.pad(k, ((0, 0), (0, pad), (0, 0), (0, 0)))
    v = jnp.pad(v, ((0, 0), (0, pad), (0, 0), (0, 0)))
    kb = k.reshape(bsz, n_blocks, MOBA_BLOCK, B_HEADS, B_HEAD_DIM).transpose(0, 3, 1, 2, 4).astype(F32)
    vb = v.reshape(bsz, n_blocks, MOBA_BLOCK, B_HEADS, B_HEAD_DIM).transpose(0, 3, 1, 2, 4).astype(F32)
    kmean = jnp.mean(kb, axis=3)
    return kb, vb, kmean


def moba_mixer(u, w_in, w_out, kb, vb, kmean, cos, sin):
    bsz, T, _ = u.shape
    n_blocks = kb.shape[2]
    n_qblocks = T // Q_BLOCK
    k_past = min(MOBA_TOPK, n_blocks)
    n_slots = k_past + 1
    scale = B_HEAD_DIM ** -0.5
    q, z = jnp.split(u @ w_in, 2, axis=-1)
    q = apply_rope(q.reshape(bsz, T, B_HEADS, B_HEAD_DIM), cos, sin)
    qblocks = q.reshape(bsz, n_qblocks, Q_BLOCK, B_HEADS, B_HEAD_DIM).transpose(1, 0, 3, 2, 4)
    offs = jnp.arange(MOBA_BLOCK)
    is_own = jnp.arange(n_slots) == k_past

    def attend(args):
        qi, qb = args
        qf = qb.astype(F32)
        qpos = qi * Q_BLOCK + jnp.arange(Q_BLOCK)
        j = (qi * Q_BLOCK) // MOBA_BLOCK
        gate = jnp.einsum('bhqd,bhnd->bhqn', qf, kmean)
        gate = jnp.where(jnp.arange(n_blocks) < j, gate, -jnp.inf)
        _, top = lax.top_k(gate, k_past)
        idx = jnp.concatenate([top, jnp.full(top.shape[:-1] + (1,), j, top.dtype)], axis=-1)
        logits_blk = jnp.einsum('bhqd,bhnkd->bhqnk', qf, kb) * scale
        logits = jnp.take_along_axis(logits_blk, idx[..., None], axis=3)
        kpos = idx[..., None] * MOBA_BLOCK + offs
        valid = (kpos <= qpos[:, None, None]) & (is_own[:, None] | (idx < j)[..., None])
        logits = jnp.where(valid, logits, -jnp.inf)
        p = jax.nn.softmax(logits.reshape(logits.shape[:3] + (n_slots * MOBA_BLOCK,)), axis=-1)
        p = p.reshape(logits.shape)
        p_blk = jnp.einsum('bhqsk,bhqsn->bhqnk', p, jax.nn.one_hot(idx, n_blocks, dtype=F32))
        return jnp.einsum('bhqnk,bhnkd->bhqd', p_blk, vb)

    o = lax.map(attend, (jnp.arange(n_qblocks), qblocks))
    o = o.transpose(1, 0, 3, 2, 4).reshape(bsz, T, B_HEADS * B_HEAD_DIM)
    o = o * jax.nn.silu(z.astype(F32))
    return o.astype(u.dtype) @ w_out


def setup_inputs(seed: int = 0) -> dict:
    key = jax.random.key(seed)
    ks = jax.random.split(key, 16)
    D = D_MODEL
    s = D ** -0.5
    nrm = lambda k, shape, sc: jax.random.normal(k, shape, F32) * sc
    x = nrm(ks[0], (BATCH, SEQ, D), 1.0)
    c = nrm(ks[1], (BATCH, D), 1.0)
    positions = (jax.random.randint(ks[2], (BATCH, 1), 0, 1024) + jnp.arange(SEQ)[None, :]).astype(jnp.int32)
    mod_w = nrm(ks[3], (DEPTH, D, 3 * D), 0.5 * s)
    mod_b = nrm(ks[4], (DEPTH, 3 * D), 0.02)
    pre_norm_g = 1.0 + nrm(ks[5], (DEPTH, D), 0.02)
    post_norm_g = 1.0 + nrm(ks[6], (DEPTH, D), 0.02)
    a_w_in = nrm(ks[7], (N_A_LAYERS, D, 2 * A_F_DIM + 2 * A_I_DIM), s)
    a_w_out = nrm(ks[8], (N_A_LAYERS, A_I_DIM, D), A_I_DIM ** -0.5)
    a_out_norm_g = 1.0 + nrm(ks[9], (N_A_LAYERS, A_VAL_DIM), 0.02)
    a_lb_logits = nrm(ks[10], (N_A_LAYERS + 1, A_F_DIM), 0.5)
    kv_norm_g = 1.0 + nrm(ks[11], (D,), 0.02)
    w_kv = nrm(ks[12], (D, 2 * B_HEADS * B_HEAD_DIM), s)
    b_w_in = nrm(ks[13], (N_B_LAYERS, D, 2 * B_HEADS * B_HEAD_DIM), s)
    b_w_out = nrm(ks[14], (N_B_LAYERS, B_HEADS * B_HEAD_DIM, D), (B_HEADS * B_HEAD_DIM) ** -0.5)
    return {"x": x, "c": c, "positions": positions, "mod_w": mod_w, "mod_b": mod_b,
            "pre_norm_g": pre_norm_g, "post_norm_g": post_norm_g, "a_w_in": a_w_in,
            "a_w_out": a_w_out, "a_out_norm_g": a_out_norm_g, "a_lb_logits": a_lb_logits,
            "kv_norm_g": kv_norm_g, "w_kv": w_kv, "b_w_in": b_w_in, "b_w_out": b_w_out}


def reference(x, c, positions, mod_w, mod_b, pre_norm_g, post_norm_g, a_w_in, a_w_out,
              a_out_norm_g, a_lb_logits, kv_norm_g, w_kv, b_w_in, b_w_out):
    cos, sin = rope_tables(positions)
    lbs = jnp.cumsum(jax.nn.softmax(a_lb_logits.astype(F32), axis=0), axis=0)
    h = x
    kb = vb = kmean = None
    for layer in range(DEPTH):
        shift, scale, gate = modulation(c, mod_w[layer], mod_b[layer])
        u = rms_norm(h, pre_norm_g[layer]) * (1.0 + scale) + shift
        if layer < N_A_LAYERS:
            y = hgrn2_mixer(u, a_w_in[layer], a_w_out[layer], a_out_norm_g[layer], lbs[layer])
        else:
            if layer == N_A_LAYERS:
                kb, vb, kmean = shared_kv(h, kv_norm_g, w_kv, cos, sin)
            li = layer - N_A_LAYERS
            y = moba_mixer(u, b_w_in[li], b_w_out[li], kb, vb, kmean, cos, sin)
        h = h + gate * rms_norm(y, post_norm_g[layer])
    return h
```

```python
import math

import jax
import jax.numpy as jnp
from jax import lax
from jax.experimental import pallas as pl
from jax.experimental.pallas import tpu as pltpu

F32 = jnp.float32
BF16 = jnp.bfloat16

NORM_EPS = 1e-6
ROPE_THETA = 10000.0
HEADS = 8
HEAD_DIM = 128
MOBA_BLOCK = 256
MOBA_TOPK = 3
TILE = 256
BASE = 8
EXP_CLAMP = 80.0
NEG = -1e30
VMEM_LIMIT = 56 * 1024 * 1024


def _nt(a, b):
    return lax.dot_general(a, b, (((1,), (1,)), ((), ())), preferred_element_type=F32)


def _tn(a, b):
    return lax.dot_general(a, b, (((0,), (0,)), ((), ())), preferred_element_type=F32)


def _dot(a, b):
    return jnp.dot(a, b, preferred_element_type=F32)


def _split(x):
    hi = x.astype(BF16)
    lo = (x - hi.astype(F32)).astype(BF16)
    return hi, lo


def _silu(x):
    return x * (1.0 / (1.0 + jnp.exp(-x)))


def _rms(x):
    return x * lax.rsqrt(jnp.mean(x * x, axis=-1, keepdims=True) + NORM_EPS)


def _mod_kernel(c_ref, w_ref, b_ref, o_ref):
    a = _silu(c_ref[...])
    ah, al = _split(a)
    wh, wl = _split(w_ref[0])
    o_ref[0] = _dot(ah, wh) + (_dot(ah, wl) + _dot(al, wh)) + b_ref[0]


def _modulation(c, mod_w, mod_b):
    depth, d, d3 = mod_w.shape
    bsz = c.shape[0]
    nb = d3 // d
    return pl.pallas_call(
        _mod_kernel,
        grid=(depth, nb),
        in_specs=[
            pl.BlockSpec((bsz, d), lambda l, j: (0, 0)),
            pl.BlockSpec((1, d, d), lambda l, j: (l, 0, j)),
            pl.BlockSpec((1, 1, d), lambda l, j: (l, 0, j)),
        ],
        out_specs=pl.BlockSpec((1, bsz, d), lambda l, j: (l, 0, j)),
        out_shape=jax.ShapeDtypeStruct((depth, bsz, d3), F32),
        compiler_params=pltpu.CompilerParams(vmem_limit_bytes=VMEM_LIMIT),
        name="modulation",
    )(c, mod_w, mod_b.reshape(depth, 1, d3))


def _rope_kernel(pos_ref, cos_ref, sin_ref):
    pos = pos_ref[0].astype(F32)
    lane = lax.broadcasted_iota(jnp.int32, (1, HEAD_DIM), 1)
    half = HEAD_DIM // 2
    i = (lane & (half - 1)).astype(F32)
    inv = jnp.exp(i * (-2.0 / HEAD_DIM * math.log(ROPE_THETA)))
    ang = pos * inv
    cos_ref[0] = jnp.cos(ang)
    s = jnp.sin(ang)
    sin_ref[0] = jnp.where(lane < half, -s, s)


def _rope_tables(positions):
    bsz, t = positions.shape
    out = jax.ShapeDtypeStruct((bsz, t, HEAD_DIM), F32)
    return pl.pallas_call(
        _rope_kernel,
        grid=(bsz,),
        in_specs=[pl.BlockSpec((1, t, 1), lambda b: (b, 0, 0))],
        out_specs=[pl.BlockSpec((1, t, HEAD_DIM), lambda b: (b, 0, 0))] * 2,
        out_shape=[out, out],
        compiler_params=pltpu.CompilerParams(vmem_limit_bytes=VMEM_LIMIT),
        name="rope_tables",
    )(positions.reshape(bsz, t, 1))


def _rope(x, cosf, sinf):
    return x * cosf + pltpu.roll(x, HEAD_DIM // 2, 1) * sinf


def _pair_levels():
    r = lax.broadcasted_iota(jnp.int32, (TILE, TILE), 0)
    c = lax.broadcasted_iota(jnp.int32, (TILE, TILE), 1)
    x = r ^ c
    lvl = jnp.zeros((TILE, TILE), jnp.int32)
    m = BASE
    while m < TILE:
        lvl = lvl + (x >= m).astype(jnp.int32)
        m *= 2
    return jnp.where(c <= r, lvl, -1)


def _hgrn2_head(q, k, v, b, st, lvl):
    nb = TILE // BASE
    b3 = b.reshape(nb, BASE, HEAD_DIM)
    e = (b3 - b3[:, BASE // 2 - 1:BASE // 2, :]).reshape(TILE, HEAD_DIM)
    xq = jnp.exp(jnp.minimum(e, EXP_CLAMP))
    xk = jnp.exp(jnp.minimum(-e, EXP_CLAMP))
    p = _nt((q * xq).astype(BF16), (k * xk).astype(BF16))
    att = jnp.where(lvl == 0, p, 0.0)
    m = BASE
    level = 1
    while m < TILE:
        nb = TILE // (2 * m)
        b3 = b.reshape(nb, 2 * m, HEAD_DIM)
        e = (b3 - b3[:, m - 1:m, :]).reshape(TILE, HEAD_DIM)
        x = jnp.exp(-jnp.abs(e))
        p = _nt((q * x).astype(BF16), (k * x).astype(BF16))
        att = att + jnp.where(lvl == level, p, 0.0)
        m *= 2
        level += 1
    o = _dot(att.astype(BF16), v) + _nt((q * jnp.exp(b)).astype(BF16), st.astype(BF16))
    b_end = b[TILE - 1:TILE, :]
    kw = (k * jnp.exp(b_end - b)).astype(BF16)
    st_new = st * jnp.exp(b_end) + _tn(v, kw)
    return o, st_new


def _layer0_kernel(x_ref, mod_ref, png_ref, win_ref, wout_ref, ong_ref, lbl_ref, pong_ref,
                   o_ref, st_ref, og_ref):
    @pl.when(pl.program_id(1) == 0)
    def _():
        st_ref[...] = jnp.zeros_like(st_ref)

    d = x_ref.shape[-1]
    x = x_ref[0]
    mod = mod_ref[0]
    shift, scale, gate = mod[:, :d], mod[:, d:2 * d], mod[:, 2 * d:]
    u = (_rms(x) * png_ref[...]) * (1.0 + scale) + shift
    ub = u.astype(BF16)

    lbl = lbl_ref[...]
    ex = jnp.exp(lbl - jnp.max(lbl, axis=0, keepdims=True))
    lb = ex[0:1, :] / jnp.sum(ex, axis=0, keepdims=True)

    qa = _silu(_dot(ub, win_ref[:, 0:d]))
    fg = lb + (1.0 - lb) * (1.0 / (1.0 + jnp.exp(-_dot(ub, win_ref[:, d:2 * d]))))
    ka = 1.0 - fg
    lf_hi, lf_lo = _split(jnp.log(fg))
    r = lax.broadcasted_iota(jnp.int32, (TILE, TILE), 0)
    c = lax.broadcasted_iota(jnp.int32, (TILE, TILE), 1)
    tri = (c <= r).astype(BF16)
    bcum = _dot(tri, lf_hi) + _dot(tri, lf_lo)
    va = _dot(ub, win_ref[:, 2 * d:3 * d]).astype(BF16)
    ga = _silu(_dot(ub, win_ref[:, 3 * d:4 * d]))

    lvl = _pair_levels()
    for h in range(HEADS):
        hs = slice(h * HEAD_DIM, (h + 1) * HEAD_DIM)
        o, st_new = _hgrn2_head(qa[:, hs], ka[:, hs], va[:, hs], bcum[:, hs], st_ref[h], lvl)
        st_ref[h] = st_new
        og_ref[:, hs] = ((_rms(o) * ong_ref[...]) * ga[:, hs]).astype(BF16)

    y = _dot(og_ref[...], wout_ref[...])
    o_ref[0] = x + gate * (_rms(y) * pong_ref[...])


def _layer0(x, mod, pre_g, w_in, w_out, out_g, lb_logits, post_g):
    bsz, t, d = x.shape
    nt = t // TILE
    const = lambda *shape: pl.BlockSpec(shape, lambda b, i: (0,) * len(shape))
    return pl.pallas_call(
        _layer0_kernel,
        grid=(bsz, nt),
        in_specs=[
            pl.BlockSpec((1, TILE, d), lambda b, i: (b, i, 0)),
            pl.BlockSpec((1, 1, 3 * d), lambda b, i: (b, 0, 0)),
            const(1, d),
            const(d, 4 * d),
            const(d, d),
            const(1, HEAD_DIM),
            const(lb_logits.shape[0], d),
            const(1, d),
        ],
        out_specs=pl.BlockSpec((1, TILE, d), lambda b, i: (b, i, 0)),
        out_shape=jax.ShapeDtypeStruct((bsz, t, d), F32),
        scratch_shapes=[
            pltpu.VMEM((HEADS, HEAD_DIM, HEAD_DIM), F32),
            pltpu.VMEM((TILE, d), BF16),
        ],
        compiler_params=pltpu.CompilerParams(
            dimension_semantics=("arbitrary", "arbitrary"), vmem_limit_bytes=VMEM_LIMIT),
        name="hgrn2_layer",
    )(x, mod.reshape(bsz, 1, 3 * d), pre_g.reshape(1, d), w_in, w_out,
      out_g.reshape(1, HEAD_DIM), lb_logits, post_g.reshape(1, d))


def _proj1_kernel(h_ref, mod_ref, kvg_ref, png_ref, wkv_ref, win_ref, cos_ref, sin_ref,
                  k_ref, v_ref, q_ref, z_ref, km_ref):
    d = h_ref.shape[-1]
    xn = _rms(h_ref[0])
    mod = mod_ref[0]
    shift, scale = mod[:, :d], mod[:, d:2 * d]
    a = (xn * kvg_ref[...]).astype(BF16)
    u = ((xn * png_ref[...]) * (1.0 + scale) + shift).astype(BF16)
    cosf, sinf = cos_ref[0], sin_ref[0]
    kk = _dot(a, wkv_ref[:, 0:d])
    qq = _dot(u, win_ref[:, 0:d])
    qscale = HEAD_DIM ** -0.5
    for h in range(HEADS):
        hs = slice(h * HEAD_DIM, (h + 1) * HEAD_DIM)
        kr = _rope(kk[:, hs], cosf, sinf)
        k_ref[0, :, hs] = kr.astype(BF16)
        km_ref[0, 0, :, hs] = jnp.mean(kr, axis=0, keepdims=True)
        q_ref[0, :, hs] = (_rope(qq[:, hs], cosf, sinf) * qscale).astype(BF16)
    v_ref[0] = _dot(a, wkv_ref[:, d:2 * d]).astype(BF16)
    z_ref[0] = _dot(u, win_ref[:, d:2 * d]).astype(BF16)


def _proj1(h, mod, kv_g, pre_g, w_kv, w_in, cosf, sinf):
    bsz, t, d = h.shape
    nt = t // TILE
    const = lambda *shape: pl.BlockSpec(shape, lambda b, i: (0,) * len(shape))
    tile = pl.BlockSpec((1, TILE, d), lambda b, i: (b, i, 0))
    rope = pl.BlockSpec((1, TILE, HEAD_DIM), lambda b, i: (b, i, 0))
    act = jax.ShapeDtypeStruct((bsz, t, d), BF16)
    return pl.pallas_call(
        _proj1_kernel,
        grid=(bsz, nt),
        in_specs=[tile, pl.BlockSpec((1, 1, 3 * d), lambda b, i: (b, 0, 0)),
                  const(1, d), const(1, d), const(d, 2 * d), const(d, 2 * d), rope, rope],
        out_specs=[tile, tile, tile, tile,
                   pl.BlockSpec((1, 1, 1, d), lambda b, i: (b, i, 0, 0))],
        out_shape=[act, act, act, act, jax.ShapeDtypeStruct((bsz, nt, 1, d), F32)],
        compiler_params=pltpu.CompilerParams(
            dimension_semantics=("arbitrary", "arbitrary"), vmem_limit_bytes=VMEM_LIMIT),
        name="moba_proj",
    )(h, mod.reshape(bsz, 1, 3 * d), kv_g.reshape(1, d), pre_g.reshape(1, d), w_kv, w_in, cosf, sinf)


def _moba_kernel(q_ref, k_ref, v_ref, km_ref, z_ref, h_ref, mod_ref, wout_ref, pong_ref,
                 o_ref, sb_ref, og_ref):
    d = h_ref.shape[-1]
    i = pl.program_id(1)
    nblk = km_ref.shape[1]
    r = lax.broadcasted_iota(jnp.int32, (TILE, MOBA_BLOCK), 0)
    c = lax.broadcasted_iota(jnp.int32, (TILE, MOBA_BLOCK), 1)
    causal = c <= r
    blk = lax.broadcasted_iota(jnp.int32, (TILE, nblk), 1)
    row0 = pl.multiple_of(i * MOBA_BLOCK, MOBA_BLOCK)

    for h in range(HEADS):
        hs = slice(h * HEAD_DIM, (h + 1) * HEAD_DIM)
        q = q_ref[0, :, hs]
        km_hi, km_lo = _split(km_ref[0, :, hs])
        g = _nt(q, km_hi) + _nt(q, km_lo)
        g = jnp.where(blk < i, g, -jnp.inf)
        rank = jnp.zeros((TILE, nblk), jnp.int32)
        for m in range(nblk):
            gm = g[:, m:m + 1]
            rank = rank + ((gm > g) | ((gm == g) & (m < blk))).astype(jnp.int32)
        bias = jnp.where(rank < MOBA_TOPK, 0.0, NEG)
        for n in range(nblk):
            sb_ref[n] = jnp.broadcast_to(bias[:, n:n + 1], (TILE, MOBA_BLOCK))

        s = _nt(q, k_ref[0, pl.ds(row0, MOBA_BLOCK), hs])
        s = jnp.where(causal, s, NEG)
        m0 = jnp.max(s, axis=-1, keepdims=True)
        p = jnp.exp(s - m0)
        l0 = jnp.sum(p, axis=-1, keepdims=True)
        acc0 = _dot(p.astype(BF16), v_ref[0, pl.ds(row0, MOBA_BLOCK), hs])

        def body(n, carry):
            m_run, l_run, acc = carry
            rows = pl.ds(pl.multiple_of(n * MOBA_BLOCK, MOBA_BLOCK), MOBA_BLOCK)
            s = _nt(q, k_ref[0, rows, hs]) + sb_ref[n]
            m_new = jnp.maximum(m_run, jnp.max(s, axis=-1, keepdims=True))
            alpha = jnp.exp(m_run - m_new)
            p = jnp.exp(s - m_new)
            l_new = alpha * l_run + jnp.sum(p, axis=-1, keepdims=True)
            acc = alpha * acc + _dot(p.astype(BF16), v_ref[0, rows, hs])
            return m_new, l_new, acc

        _, l_fin, acc = lax.fori_loop(0, i, body, (m0, l0, acc0))
        o = acc / l_fin
        og_ref[:, hs] = (o * _silu(z_ref[0, :, hs].astype(F32))).astype(BF16)

    y = _dot(og_ref[...], wout_ref[...])
    gate = mod_ref[0][:, 2 * d:]
    o_ref[0] = h_ref[0] + gate * (_rms(y) * pong_ref[...])


def _moba(q, k, v, kmean, z, h, mod, w_out, post_g):
    bsz, t, d = h.shape
    nt = t // TILE
    const = lambda *shape: pl.BlockSpec(shape, lambda b, i: (0,) * len(shape))
    tile = pl.BlockSpec((1, TILE, d), lambda b, i: (b, i, 0))
    seq = pl.BlockSpec((1, t, d), lambda b, i: (b, 0, 0))
    return pl.pallas_call(
        _moba_kernel,
        grid=(bsz, nt),
        in_specs=[tile, seq, seq, pl.BlockSpec((1, nt, d), lambda b, i: (b, 0, 0)),
                  tile, tile, pl.BlockSpec((1, 1, 3 * d), lambda b, i: (b, 0, 0)),
                  const(d, d), const(1, d)],
        out_specs=tile,
        out_shape=jax.ShapeDtypeStruct((bsz, t, d), F32),
        scratch_shapes=[
            pltpu.VMEM((t // MOBA_BLOCK, TILE, MOBA_BLOCK), F32),
            pltpu.VMEM((TILE, d), BF16),
        ],
        compiler_params=pltpu.CompilerParams(
            dimension_semantics=("arbitrary", "arbitrary"), vmem_limit_bytes=VMEM_LIMIT),
        name="moba_attention",
    )(q, k, v, kmean, z, h, mod.reshape(bsz, 1, 3 * d), w_out, post_g.reshape(1, d))


def kernel(x, c, positions, mod_w, mod_b, pre_norm_g, post_norm_g, a_w_in, a_w_out, a_out_norm_g,
           a_lb_logits, kv_norm_g, w_kv, b_w_in, b_w_out):
    assert x.shape[1] % TILE == 0 and x.shape[2] == HEADS * HEAD_DIM
    assert mod_w.shape[0] == 2 and a_w_in.shape[0] == 1 and b_w_in.shape[0] == 1
    mod = _modulation(c, mod_w, mod_b)
    cosf, sinf = _rope_tables(positions)
    h1 = _layer0(x, mod[0], pre_norm_g[0], a_w_in[0].astype(BF16), a_w_out[0].astype(BF16),
                 a_out_norm_g[0], a_lb_logits, post_norm_g[0])
    k, v, q, z, kmean = _proj1(h1, mod[1], kv_norm_g, pre_norm_g[1], w_kv.astype(BF16),
                               b_w_in[0].astype(BF16), cosf, sinf)
    kmean = kmean.reshape(kmean.shape[0], kmean.shape[1], kmean.shape[3])
    return _moba(q, k, v, kmean, z, h1, mod[1], b_w_out[0].astype(BF16), post_norm_g[1])
```

```python
import math

import jax
import jax.numpy as jnp
from jax import lax
from jax.experimental import pallas as pl
from jax.experimental.pallas import tpu as pltpu

F32 = jnp.float32
BF16 = jnp.bfloat16

NORM_EPS = 1e-6
ROPE_THETA = 10000.0
HEADS = 8
HEAD_DIM = 128
MOBA_BLOCK = 256
MOBA_TOPK = 3
TILE = 256
BASE = 8
EXP_CLAMP = 80.0
NEG = -1e30
VMEM_LIMIT = 56 * 1024 * 1024


def _nt(a, b):
    return lax.dot_general(a, b, (((1,), (1,)), ((), ())), preferred_element_type=F32)


def _tn(a, b):
    return lax.dot_general(a, b, (((0,), (0,)), ((), ())), preferred_element_type=F32)


def _dot(a, b):
    return jnp.dot(a, b, preferred_element_type=F32)


def _split(x):
    hi = x.astype(BF16)
    lo = (x - hi.astype(F32)).astype(BF16)
    return hi, lo


def _silu(x):
    return x * (1.0 / (1.0 + jnp.exp(-x)))


def _rms(x):
    return x * lax.rsqrt(jnp.mean(x * x, axis=-1, keepdims=True) + NORM_EPS)


def _mod_kernel(c_ref, w_ref, b_ref, o_ref):
    a = _silu(c_ref[...])
    ah, al = _split(a)
    wh, wl = _split(w_ref[0])
    o_ref[0] = _dot(ah, wh) + (_dot(ah, wl) + _dot(al, wh)) + b_ref[0]


def _modulation(c, mod_w, mod_b):
    depth, d, d3 = mod_w.shape
    bsz = c.shape[0]
    nb = d3 // d
    return pl.pallas_call(
        _mod_kernel,
        grid=(depth, nb),
        in_specs=[
            pl.BlockSpec((bsz, d), lambda l, j: (0, 0)),
            pl.BlockSpec((1, d, d), lambda l, j: (l, 0, j)),
            pl.BlockSpec((1, 1, d), lambda l, j: (l, 0, j)),
        ],
        out_specs=pl.BlockSpec((1, bsz, d), lambda l, j: (l, 0, j)),
        out_shape=jax.ShapeDtypeStruct((depth, bsz, d3), F32),
        compiler_params=pltpu.CompilerParams(vmem_limit_bytes=VMEM_LIMIT),
        name="modulation",
    )(c, mod_w, mod_b.reshape(depth, 1, d3))


def _rope_kernel(pos_ref, cos_ref, sin_ref):
    pos = pos_ref[0].astype(F32)
    lane = lax.broadcasted_iota(jnp.int32, (1, HEAD_DIM), 1)
    half = HEAD_DIM // 2
    i = (lane & (half - 1)).astype(F32)
    inv = jnp.exp(i * (-2.0 / HEAD_DIM * math.log(ROPE_THETA)))
    ang = pos * inv
    cos_ref[0] = jnp.cos(ang)
    s = jnp.sin(ang)
    sin_ref[0] = jnp.where(lane < half, -s, s)


def _rope_tables(positions):
    bsz, t = positions.shape
    out = jax.ShapeDtypeStruct((bsz, t, HEAD_DIM), F32)
    return pl.pallas_call(
        _rope_kernel,
        grid=(bsz,),
        in_specs=[pl.BlockSpec((1, t, 1), lambda b: (b, 0, 0))],
        out_specs=[pl.BlockSpec((1, t, HEAD_DIM), lambda b: (b, 0, 0))] * 2,
        out_shape=[out, out],
        compiler_params=pltpu.CompilerParams(vmem_limit_bytes=VMEM_LIMIT),
        name="rope_tables",
    )(positions.reshape(bsz, t, 1))


def _rope(x, cosf, sinf):
    return x * cosf + pltpu.roll(x, HEAD_DIM // 2, 1) * sinf


def _pair_levels():
    r = lax.broadcasted_iota(jnp.int32, (TILE, TILE), 0)
    c = lax.broadcasted_iota(jnp.int32, (TILE, TILE), 1)
    x = r ^ c
    lvl = jnp.zeros((TILE, TILE), jnp.int32)
    m = BASE
    while m < TILE:
        lvl = lvl + (x >= m).astype(jnp.int32)
        m *= 2
    return jnp.where(c <= r, lvl, -1)


def _hgrn2_head(q, k, v, b, st, lvl):
    nb = TILE // BASE
    b3 = b.reshape(nb, BASE, HEAD_DIM)
    e = (b3 - b3[:, BASE // 2 - 1:BASE // 2, :]).reshape(TILE, HEAD_DIM)
    xq = jnp.exp(jnp.minimum(e, EXP_CLAMP))
    xk = jnp.exp(jnp.minimum(-e, EXP_CLAMP))
    p = _nt((q * xq).astype(BF16), (k * xk).astype(BF16))
    att = jnp.where(lvl == 0, p, 0.0)
    m = BASE
    level = 1
    while m < TILE:
        nb = TILE // (2 * m)
        b3 = b.reshape(nb, 2 * m, HEAD_DIM)
        e = (b3 - b3[:, m - 1:m, :]).reshape(TILE, HEAD_DIM)
        x = jnp.exp(-jnp.abs(e))
        p = _nt((q * x).astype(BF16), (k * x).astype(BF16))
        att = att + jnp.where(lvl == level, p, 0.0)
        m *= 2
        level += 1
    o = _dot(att.astype(BF16), v) + _nt((q * jnp.exp(b)).astype(BF16), st.astype(BF16))
    b_end = b[TILE - 1:TILE, :]
    kw = (k * jnp.exp(b_end - b)).astype(BF16)
    st_new = st * jnp.exp(b_end) + _tn(v, kw)
    return o, st_new


def _layer0_kernel(x_ref, mod_ref, png_ref, win_ref, wout_ref, ong_ref, lbl_ref, pong_ref,
                   o_ref, st_ref, og_ref):
    @pl.when(pl.program_id(1) == 0)
    def _():
        st_ref[...] = jnp.zeros_like(st_ref)

    d = x_ref.shape[-1]
    x = x_ref[0]
    mod = mod_ref[0]
    shift, scale, gate = mod[:, :d], mod[:, d:2 * d], mod[:, 2 * d:]
    u = (_rms(x) * png_ref[...]) * (1.0 + scale) + shift
    ub = u.astype(BF16)

    lbl = lbl_ref[...]
    ex = jnp.exp(lbl - jnp.max(lbl, axis=0, keepdims=True))
    lb = ex[0:1, :] / jnp.sum(ex, axis=0, keepdims=True)

    qa = _silu(_dot(ub, win_ref[:, 0:d]))
    fg = lb + (1.0 - lb) * (1.0 / (1.0 + jnp.exp(-_dot(ub, win_ref[:, d:2 * d]))))
    ka = 1.0 - fg
    lf_hi, lf_lo = _split(jnp.log(fg))
    r = lax.broadcasted_iota(jnp.int32, (TILE, TILE), 0)
    c = lax.broadcasted_iota(jnp.int32, (TILE, TILE), 1)
    tri = (c <= r).astype(BF16)
    bcum = _dot(tri, lf_hi) + _dot(tri, lf_lo)
    va = _dot(ub, win_ref[:, 2 * d:3 * d]).astype(BF16)
    ga = _silu(_dot(ub, win_ref[:, 3 * d:4 * d]))

    lvl = _pair_levels()
    for h in range(HEADS):
        hs = slice(h * HEAD_DIM, (h + 1) * HEAD_DIM)
        o, st_new = _hgrn2_head(qa[:, hs], ka[:, hs], va[:, hs], bcum[:, hs], st_ref[h], lvl)
        st_ref[h] = st_new
        og_ref[:, hs] = ((_rms(o) * ong_ref[...]) * ga[:, hs]).astype(BF16)

    y = _dot(og_ref[...], wout_ref[...])
    o_ref[0] = x + gate * (_rms(y) * pong_ref[...])


def _layer0(x, mod, pre_g, w_in, w_out, out_g, lb_logits, post_g):
    bsz, t, d = x.shape
    nt = t // TILE
    const = lambda *shape: pl.BlockSpec(shape, lambda b, i: (0,) * len(shape))
    return pl.pallas_call(
        _layer0_kernel,
        grid=(bsz, nt),
        in_specs=[
            pl.BlockSpec((1, TILE, d), lambda b, i: (b, i, 0)),
            pl.BlockSpec((1, 1, 3 * d), lambda b, i: (b, 0, 0)),
            const(1, d),
            const(d, 4 * d),
            const(d, d),
            const(1, HEAD_DIM),
            const(lb_logits.shape[0], d),
            const(1, d),
        ],
        out_specs=pl.BlockSpec((1, TILE, d), lambda b, i: (b, i, 0)),
        out_shape=jax.ShapeDtypeStruct((bsz, t, d), F32),
        scratch_shapes=[
            pltpu.VMEM((HEADS, HEAD_DIM, HEAD_DIM), F32),
            pltpu.VMEM((TILE, d), BF16),
        ],
        compiler_params=pltpu.CompilerParams(
            dimension_semantics=("arbitrary", "arbitrary"), vmem_limit_bytes=VMEM_LIMIT),
        name="hgrn2_layer",
    )(x, mod.reshape(bsz, 1, 3 * d), pre_g.reshape(1, d), w_in, w_out,
      out_g.reshape(1, HEAD_DIM), lb_logits, post_g.reshape(1, d))


def _proj1_kernel(h_ref, mod_ref, kvg_ref, png_ref, wk_ref, wvt_ref, win_ref, cos_ref, sin_ref,
                  k_ref, vt_ref, q_ref, z_ref, km_ref):
    d = h_ref.shape[-1]
    xn = _rms(h_ref[0])
    mod = mod_ref[0]
    shift, scale = mod[:, :d], mod[:, d:2 * d]
    a = (xn * kvg_ref[...]).astype(BF16)
    u = ((xn * png_ref[...]) * (1.0 + scale) + shift).astype(BF16)
    cosf, sinf = cos_ref[0], sin_ref[0]
    kk = _dot(a, wk_ref[...])
    qq = _dot(u, win_ref[:, 0:d])
    qscale = HEAD_DIM ** -0.5
    vt = _nt(wvt_ref[...], a).astype(BF16)
    for h in range(HEADS):
        hs = slice(h * HEAD_DIM, (h + 1) * HEAD_DIM)
        kr = _rope(kk[:, hs], cosf, sinf)
        k_ref[0, h] = kr.astype(BF16)
        km_ref[0, 0, h:h + 1, :] = jnp.mean(kr, axis=0, keepdims=True)
        q_ref[0, h] = (_rope(qq[:, hs], cosf, sinf) * qscale).astype(BF16)
        vt_ref[0, h] = vt[hs, :]
    z_ref[0] = _dot(u, win_ref[:, d:2 * d]).astype(BF16)


def _proj1(h, mod, kv_g, pre_g, w_k, w_vt, w_in, cosf, sinf):
    bsz, t, d = h.shape
    nt = t // TILE
    const = lambda *shape: pl.BlockSpec(shape, lambda b, i: (0,) * len(shape))
    tile = pl.BlockSpec((1, TILE, d), lambda b, i: (b, i, 0))
    rope = pl.BlockSpec((1, TILE, HEAD_DIM), lambda b, i: (b, i, 0))
    heads = pl.BlockSpec((1, HEADS, TILE, HEAD_DIM), lambda b, i: (b, 0, i, 0))
    heads_shape = jax.ShapeDtypeStruct((bsz, HEADS, t, HEAD_DIM), BF16)
    return pl.pallas_call(
        _proj1_kernel,
        grid=(bsz, nt),
        in_specs=[tile, pl.BlockSpec((1, 1, 3 * d), lambda b, i: (b, 0, 0)),
                  const(1, d), const(1, d), const(d, d), const(d, d), const(d, 2 * d), rope, rope],
        out_specs=[heads, pl.BlockSpec((1, HEADS, HEAD_DIM, TILE), lambda b, i: (b, 0, 0, i)), heads, tile,
                   pl.BlockSpec((1, 1, HEADS, HEAD_DIM), lambda b, i: (b, i, 0, 0))],
        out_shape=[heads_shape, jax.ShapeDtypeStruct((bsz, HEADS, HEAD_DIM, t), BF16), heads_shape,
                   jax.ShapeDtypeStruct((bsz, t, d), BF16),
                   jax.ShapeDtypeStruct((bsz, nt, HEADS, HEAD_DIM), F32)],
        compiler_params=pltpu.CompilerParams(
            dimension_semantics=("arbitrary", "arbitrary"), vmem_limit_bytes=VMEM_LIMIT),
        name="moba_proj",
    )(h, mod.reshape(bsz, 1, 3 * d), kv_g.reshape(1, d), pre_g.reshape(1, d), w_k, w_vt, w_in, cosf, sinf)


HEADS_PER_ITER = 2


def _moba_head(h, slot, nb, q_ref, k_ref, vt_ref, km_ref, s_ref, p_ref, acc_ref, causal, blk):
    npast = nb - 1
    groups = (MOBA_BLOCK // 8, 8, TILE)
    q = q_ref[0, h]
    bias = None
    if npast > MOBA_TOPK:
        km_hi, km_lo = _split(km_ref[0, h])
        g = _nt(km_hi, q) + _nt(km_lo, q)
        g = jnp.where(blk < npast, g, -jnp.inf)
        rank = jnp.zeros(g.shape, jnp.int32)
        for m in range(npast):
            gm = g[m:m + 1, :]
            rank = rank + ((gm > g) | ((gm == g) & (m < blk))).astype(jnp.int32)
        bias = jnp.where(rank < MOBA_TOPK, 0.0, NEG)

    s_all = _nt(k_ref[0, h, 0:nb * MOBA_BLOCK, :], q)
    m8 = None
    for n in range(nb):
        s = s_all[n * MOBA_BLOCK:(n + 1) * MOBA_BLOCK]
        if n == npast:
            s = jnp.where(causal, s, NEG)
        elif bias is not None:
            s = s + bias[n:n + 1, :]
        s_ref[slot, n] = s
        mb = jnp.max(s.reshape(groups), axis=0)
        m8 = mb if m8 is None else jnp.maximum(m8, mb)
    m = jnp.max(m8, axis=0, keepdims=True)

    l8 = jnp.zeros(groups[1:], F32)
    for n in range(nb):
        p = jnp.exp(s_ref[slot, n] - m)
        l8 = l8 + jnp.sum(p.reshape(groups), axis=0)
        p_ref[slot, n * MOBA_BLOCK:(n + 1) * MOBA_BLOCK, :] = p.astype(BF16)
    l = jnp.sum(l8, axis=0, keepdims=True)
    acc = _dot(vt_ref[0, h, :, 0:nb * MOBA_BLOCK], p_ref[slot, 0:nb * MOBA_BLOCK, :])
    acc_ref[h] = acc * (1.0 / l)


def _moba_kernel(q_ref, k_ref, vt_ref, km_ref, z_ref, h_ref, mod_ref, wout_ref, pong_ref,
                 o_ref, s_ref, p_ref, acc_ref, og_ref):
    d = h_ref.shape[-1]
    i = pl.program_id(1)
    nblk = km_ref.shape[2]
    key = lax.broadcasted_iota(jnp.int32, (MOBA_BLOCK, TILE), 0)
    qry = lax.broadcasted_iota(jnp.int32, (MOBA_BLOCK, TILE), 1)
    causal = key <= qry
    blk = lax.broadcasted_iota(jnp.int32, (nblk, TILE), 0)

    for own in range(nblk):
        @pl.when(i == own)
        def _(own=own):
            def trip(t, carry):
                for u in range(HEADS_PER_ITER):
                    _moba_head(t * HEADS_PER_ITER + u, u, own + 1, q_ref, k_ref, vt_ref, km_ref,
                               s_ref, p_ref, acc_ref, causal, blk)
                return carry
            lax.fori_loop(0, HEADS // HEADS_PER_ITER, trip, 0)

    for h in range(HEADS):
        hs = slice(h * HEAD_DIM, (h + 1) * HEAD_DIM)
        o = acc_ref[h].T
        og_ref[:, hs] = (o * _silu(z_ref[0, :, hs].astype(F32))).astype(BF16)

    y = _dot(og_ref[...], wout_ref[...])
    gate = mod_ref[0][:, 2 * d:]
    o_ref[0] = h_ref[0] + gate * (_rms(y) * pong_ref[...])


def _moba(q, k, vt, kmean, z, h, mod, w_out, post_g):
    bsz, t, d = h.shape
    nt = t // TILE
    const = lambda *shape: pl.BlockSpec(shape, lambda b, i: (0,) * len(shape))
    tile = pl.BlockSpec((1, TILE, d), lambda b, i: (b, i, 0))
    return pl.pallas_call(
        _moba_kernel,
        grid=(bsz, nt),
        in_specs=[pl.BlockSpec((1, HEADS, TILE, HEAD_DIM), lambda b, i: (b, 0, i, 0)),
                  pl.BlockSpec((1, HEADS, t, HEAD_DIM), lambda b, i: (b, 0, 0, 0)),
                  pl.BlockSpec((1, HEADS, HEAD_DIM, t), lambda b, i: (b, 0, 0, 0)),
                  pl.BlockSpec((1, HEADS, nt, HEAD_DIM), lambda b, i: (b, 0, 0, 0)),
                  tile, tile, pl.BlockSpec((1, 1, 3 * d), lambda b, i: (b, 0, 0)),
                  const(d, d), const(1, d)],
        out_specs=tile,
        out_shape=jax.ShapeDtypeStruct((bsz, t, d), F32),
        scratch_shapes=[
            pltpu.VMEM((HEADS_PER_ITER, nt, MOBA_BLOCK, TILE), F32),
            pltpu.VMEM((HEADS_PER_ITER, t, TILE), BF16),
            pltpu.VMEM((HEADS, HEAD_DIM, TILE), F32),
            pltpu.VMEM((TILE, d), BF16),
        ],
        compiler_params=pltpu.CompilerParams(
            dimension_semantics=("arbitrary", "arbitrary"), vmem_limit_bytes=VMEM_LIMIT),
        name="moba_attention",
    )(q, k, vt, kmean, z, h, mod.reshape(bsz, 1, 3 * d), w_out, post_g.reshape(1, d))


def kernel(x, c, positions, mod_w, mod_b, pre_norm_g, post_norm_g, a_w_in, a_w_out, a_out_norm_g,
           a_lb_logits, kv_norm_g, w_kv, b_w_in, b_w_out):
    assert x.shape[1] % TILE == 0 and x.shape[2] == HEADS * HEAD_DIM
    assert mod_w.shape[0] == 2 and a_w_in.shape[0] == 1 and b_w_in.shape[0] == 1
    mod = _modulation(c, mod_w, mod_b)
    cosf, sinf = _rope_tables(positions)
    h1 = _layer0(x, mod[0], pre_norm_g[0], a_w_in[0].astype(BF16), a_w_out[0].astype(BF16),
                 a_out_norm_g[0], a_lb_logits, post_norm_g[0])
    d = x.shape[2]
    k, vt, q, z, kmean = _proj1(h1, mod[1], kv_norm_g, pre_norm_g[1], w_kv[:, :d].astype(BF16),
                                w_kv[:, d:].T.astype(BF16), b_w_in[0].astype(BF16), cosf, sinf)
    kmean = jnp.transpose(kmean, (0, 2, 1, 3))
    return _moba(q, k, vt, kmean, z, h1, mod[1], b_w_out[0].astype(BF16), post_norm_g[1])
```

```python
import math

import jax
import jax.numpy as jnp
from jax import lax
from jax.experimental import pallas as pl
from jax.experimental.pallas import tpu as pltpu

F32 = jnp.float32
BF16 = jnp.bfloat16

NORM_EPS = 1e-6
ROPE_THETA = 10000.0
HEADS = 8
HEAD_DIM = 128
MOBA_BLOCK = 256
MOBA_TOPK = 3
TILE = 256
BASE = 8
EXP2_CLAMP = 115.0
NEG = -1e30
LOG2E = 1.4426950408889634
VT_ROWS = HEAD_DIM + 16
VMEM_LIMIT = 56 * 1024 * 1024


def _nt(a, b):
    return lax.dot_general(a, b, (((1,), (1,)), ((), ())), preferred_element_type=F32)


def _tn(a, b):
    return lax.dot_general(a, b, (((0,), (0,)), ((), ())), preferred_element_type=F32)


def _dot(a, b):
    return jnp.dot(a, b, preferred_element_type=F32)


def _split(x):
    hi = x.astype(BF16)
    lo = (x - hi.astype(F32)).astype(BF16)
    return hi, lo


def _silu(x):
    return x * (1.0 / (1.0 + jnp.exp(-x)))


def _rms(x):
    return x * lax.rsqrt(jnp.mean(x * x, axis=-1, keepdims=True) + NORM_EPS)


def _mod_kernel(c_ref, w_ref, b_ref, o_ref):
    a = _silu(c_ref[...])
    ah, al = _split(a)
    wh, wl = _split(w_ref[0])
    o_ref[0] = _dot(ah, wh) + (_dot(ah, wl) + _dot(al, wh)) + b_ref[0]


def _modulation(c, mod_w, mod_b):
    depth, d, d3 = mod_w.shape
    bsz = c.shape[0]
    nb = d3 // d
    return pl.pallas_call(
        _mod_kernel,
        grid=(depth, nb),
        in_specs=[
            pl.BlockSpec((bsz, d), lambda l, j: (0, 0)),
            pl.BlockSpec((1, d, d), lambda l, j: (l, 0, j)),
            pl.BlockSpec((1, 1, d), lambda l, j: (l, 0, j)),
        ],
        out_specs=pl.BlockSpec((1, bsz, d), lambda l, j: (l, 0, j)),
        out_shape=jax.ShapeDtypeStruct((depth, bsz, d3), F32),
        compiler_params=pltpu.CompilerParams(vmem_limit_bytes=VMEM_LIMIT),
        name="modulation",
    )(c, mod_w, mod_b.reshape(depth, 1, d3))


def _rope_kernel(pos_ref, cos_ref, sin_ref):
    pos = pos_ref[0].astype(F32)
    lane = lax.broadcasted_iota(jnp.int32, (1, HEAD_DIM), 1)
    half = HEAD_DIM // 2
    i = (lane & (half - 1)).astype(F32)
    inv = jnp.exp(i * (-2.0 / HEAD_DIM * math.log(ROPE_THETA)))
    ang = pos * inv
    cos_ref[0] = jnp.cos(ang)
    s = jnp.sin(ang)
    sin_ref[0] = jnp.where(lane < half, -s, s)


def _rope_tables(positions):
    bsz, t = positions.shape
    out = jax.ShapeDtypeStruct((bsz, t, HEAD_DIM), F32)
    return pl.pallas_call(
        _rope_kernel,
        grid=(bsz,),
        in_specs=[pl.BlockSpec((1, t, 1), lambda b: (b, 0, 0))],
        out_specs=[pl.BlockSpec((1, t, HEAD_DIM), lambda b: (b, 0, 0))] * 2,
        out_shape=[out, out],
        compiler_params=pltpu.CompilerParams(vmem_limit_bytes=VMEM_LIMIT),
        name="rope_tables",
    )(positions.reshape(bsz, t, 1))


def _rope(x, cosf, sinf):
    return x * cosf + pltpu.roll(x, HEAD_DIM // 2, 1) * sinf


def _pair_levels():
    r = lax.broadcasted_iota(jnp.int32, (TILE, TILE), 0)
    c = lax.broadcasted_iota(jnp.int32, (TILE, TILE), 1)
    x = r ^ c
    lvl = jnp.zeros((TILE, TILE), jnp.int32)
    m = BASE
    while m < TILE:
        lvl = lvl + (x >= m).astype(jnp.int32)
        m *= 2
    return jnp.where(c <= r, lvl, -1)


def _hgrn2_head(q, k, v, b, st, lvl):
    nb = TILE // BASE
    b3 = b.reshape(nb, BASE, HEAD_DIM)
    e = (b3 - b3[:, BASE // 2 - 1:BASE // 2, :]).reshape(TILE, HEAD_DIM)
    xq = jnp.exp2(jnp.minimum(e, EXP2_CLAMP))
    xk = jnp.exp2(jnp.minimum(-e, EXP2_CLAMP))
    p = _nt((q * xq).astype(BF16), (k * xk).astype(BF16))
    att = jnp.where(lvl == 0, p, 0.0)
    m = BASE
    level = 1
    while m < TILE:
        nb = TILE // (2 * m)
        b3 = b.reshape(nb, 2 * m, HEAD_DIM)
        e = (b3 - b3[:, m - 1:m, :]).reshape(TILE, HEAD_DIM)
        x = jnp.exp2(-jnp.abs(e))
        p = _nt((q * x).astype(BF16), (k * x).astype(BF16))
        att = att + jnp.where(lvl == level, p, 0.0)
        m *= 2
        level += 1
    o = _dot(att.astype(BF16), v) + _nt((q * jnp.exp2(b)).astype(BF16), st.astype(BF16))
    b_end = b[TILE - 1:TILE, :]
    kw = (k * jnp.exp2(b_end - b)).astype(BF16)
    st_new = st * jnp.exp2(b_end) + _tn(v, kw)
    return o, st_new


def _layer0_kernel(x_ref, mod_ref, png_ref, win_ref, wout_ref, ong_ref, lbl_ref, pong_ref,
                   o_ref, st_ref, og_ref):
    @pl.when(pl.program_id(1) == 0)
    def _():
        st_ref[...] = jnp.zeros_like(st_ref)

    d = x_ref.shape[-1]
    x = x_ref[0]
    mod = mod_ref[0]
    shift, scale, gate = mod[:, :d], mod[:, d:2 * d], mod[:, 2 * d:]
    u = (_rms(x) * png_ref[...]) * (1.0 + scale) + shift
    ub = u.astype(BF16)

    lbl = lbl_ref[...]
    ex = jnp.exp(lbl - jnp.max(lbl, axis=0, keepdims=True))
    lb = ex[0:1, :] / jnp.sum(ex, axis=0, keepdims=True)

    qa = _silu(_dot(ub, win_ref[:, 0:d]))
    fg = lb + (1.0 - lb) * (1.0 / (1.0 + jnp.exp(-_dot(ub, win_ref[:, d:2 * d]))))
    ka = 1.0 - fg
    lf_hi, lf_lo = _split(jnp.log2(fg))
    r = lax.broadcasted_iota(jnp.int32, (TILE, TILE), 0)
    c = lax.broadcasted_iota(jnp.int32, (TILE, TILE), 1)
    tri = (c <= r).astype(BF16)
    bcum = _dot(tri, lf_hi) + _dot(tri, lf_lo)
    va = _dot(ub, win_ref[:, 2 * d:3 * d]).astype(BF16)
    ga = _silu(_dot(ub, win_ref[:, 3 * d:4 * d]))

    lvl = _pair_levels()
    for h in range(HEADS):
        hs = slice(h * HEAD_DIM, (h + 1) * HEAD_DIM)
        o, st_new = _hgrn2_head(qa[:, hs], ka[:, hs], va[:, hs], bcum[:, hs], st_ref[h], lvl)
        st_ref[h] = st_new
        og_ref[:, hs] = ((_rms(o) * ong_ref[...]) * ga[:, hs]).astype(BF16)

    y = _dot(og_ref[...], wout_ref[...])
    o_ref[0] = x + gate * (_rms(y) * pong_ref[...])


def _layer0(x, mod, pre_g, w_in, w_out, out_g, lb_logits, post_g):
    bsz, t, d = x.shape
    nt = t // TILE
    const = lambda *shape: pl.BlockSpec(shape, lambda b, i: (0,) * len(shape))
    return pl.pallas_call(
        _layer0_kernel,
        grid=(bsz, nt),
        in_specs=[
            pl.BlockSpec((1, TILE, d), lambda b, i: (b, i, 0)),
            pl.BlockSpec((1, 1, 3 * d), lambda b, i: (b, 0, 0)),
            const(1, d),
            const(d, 4 * d),
            const(d, d),
            const(1, HEAD_DIM),
            const(lb_logits.shape[0], d),
            const(1, d),
        ],
        out_specs=pl.BlockSpec((1, TILE, d), lambda b, i: (b, i, 0)),
        out_shape=jax.ShapeDtypeStruct((bsz, t, d), F32),
        scratch_shapes=[
            pltpu.VMEM((HEADS, HEAD_DIM, HEAD_DIM), F32),
            pltpu.VMEM((TILE, d), BF16),
        ],
        compiler_params=pltpu.CompilerParams(
            dimension_semantics=("arbitrary", "arbitrary"), vmem_limit_bytes=VMEM_LIMIT),
        name="hgrn2_layer",
    )(x, mod.reshape(bsz, 1, 3 * d), pre_g.reshape(1, d), w_in, w_out,
      out_g.reshape(1, HEAD_DIM), lb_logits, post_g.reshape(1, d))


def _proj1_kernel(h_ref, mod_ref, kvg_ref, png_ref, wk_ref, wvt_ref, win_ref, cos_ref, sin_ref,
                  k_ref, vt_ref, q_ref, z_ref, km_ref):
    d = h_ref.shape[-1]
    xn = _rms(h_ref[0])
    mod = mod_ref[0]
    shift, scale = mod[:, :d], mod[:, d:2 * d]
    a = (xn * kvg_ref[...]).astype(BF16)
    u = ((xn * png_ref[...]) * (1.0 + scale) + shift).astype(BF16)
    cosf, sinf = cos_ref[0], sin_ref[0]
    kk = _dot(a, wk_ref[...])
    qq = _dot(u, win_ref[:, 0:d])
    qscale = HEAD_DIM ** -0.5 * LOG2E
    vt = _nt(wvt_ref[...], a).astype(BF16)
    ones = jnp.ones((VT_ROWS - HEAD_DIM, TILE), BF16)
    for h in range(HEADS):
        hs = slice(h * HEAD_DIM, (h + 1) * HEAD_DIM)
        kr = _rope(kk[:, hs], cosf, sinf)
        k_ref[0, h] = kr.astype(BF16)
        km_ref[0, 0, h:h + 1, :] = jnp.mean(kr, axis=0, keepdims=True)
        q_ref[0, h] = (_rope(qq[:, hs], cosf, sinf) * qscale).astype(BF16)
        vt_ref[0, h, 0:HEAD_DIM, :] = vt[hs, :]
        vt_ref[0, h, HEAD_DIM:VT_ROWS, :] = ones
    z_ref[0] = _dot(u, win_ref[:, d:2 * d]).astype(BF16)


def _proj1(h, mod, kv_g, pre_g, w_k, w_vt, w_in, cosf, sinf):
    bsz, t, d = h.shape
    nt = t // TILE
    const = lambda *shape: pl.BlockSpec(shape, lambda b, i: (0,) * len(shape))
    tile = pl.BlockSpec((1, TILE, d), lambda b, i: (b, i, 0))
    rope = pl.BlockSpec((1, TILE, HEAD_DIM), lambda b, i: (b, i, 0))
    heads = pl.BlockSpec((1, HEADS, TILE, HEAD_DIM), lambda b, i: (b, 0, i, 0))
    heads_shape = jax.ShapeDtypeStruct((bsz, HEADS, t, HEAD_DIM), BF16)
    return pl.pallas_call(
        _proj1_kernel,
        grid=(bsz, nt),
        in_specs=[tile, pl.BlockSpec((1, 1, 3 * d), lambda b, i: (b, 0, 0)),
                  const(1, d), const(1, d), const(d, d), const(d, d), const(d, 2 * d), rope, rope],
        out_specs=[heads, pl.BlockSpec((1, HEADS, VT_ROWS, TILE), lambda b, i: (b, 0, 0, i)), heads, tile,
                   pl.BlockSpec((1, 1, HEADS, HEAD_DIM), lambda b, i: (b, i, 0, 0))],
        out_shape=[heads_shape, jax.ShapeDtypeStruct((bsz, HEADS, VT_ROWS, t), BF16), heads_shape,
                   jax.ShapeDtypeStruct((bsz, t, d), BF16),
                   jax.ShapeDtypeStruct((bsz, nt, HEADS, HEAD_DIM), F32)],
        compiler_params=pltpu.CompilerParams(
            dimension_semantics=("arbitrary", "arbitrary"), vmem_limit_bytes=VMEM_LIMIT),
        name="moba_proj",
    )(h, mod.reshape(bsz, 1, 3 * d), kv_g.reshape(1, d), pre_g.reshape(1, d), w_k, w_vt, w_in, cosf, sinf)


def _moba_scores(h, nb, q_ref, k_ref, km_ref, s_ref, causal, blk):
    npast = nb - 1
    groups = (MOBA_BLOCK // 8, 8, TILE)
    q = q_ref[0, h]
    bias = None
    if npast > MOBA_TOPK:
        km_hi, km_lo = _split(km_ref[0, h])
        g = _nt(km_hi, q) + _nt(km_lo, q)
        g = jnp.where(blk < npast, g, -jnp.inf)
        rank = jnp.zeros(g.shape, jnp.int32)
        for m in range(npast):
            gm = g[m:m + 1, :]
            rank = rank + ((gm > g) | ((gm == g) & (m < blk))).astype(jnp.int32)
        bias = jnp.where(rank < MOBA_TOPK, 0.0, NEG)

    s_all = _nt(k_ref[0, h, 0:nb * MOBA_BLOCK, :], q)
    m8 = None
    for n in range(nb):
        s = s_all[n * MOBA_BLOCK:(n + 1) * MOBA_BLOCK]
        if n == npast:
            s = jnp.where(causal, s, NEG)
        elif bias is not None:
            s = s + bias[n:n + 1, :]
        s_ref[n] = s
        mb = jnp.max(s.reshape(groups), axis=0)
        m8 = mb if m8 is None else jnp.maximum(m8, mb)
    return jnp.max(m8, axis=0, keepdims=True)


def _moba_values(h, nb, m, vt_ref, s_ref, p_ref, acc_ref):
    for n in range(nb):
        p_ref[n * MOBA_BLOCK:(n + 1) * MOBA_BLOCK, :] = jnp.exp2(s_ref[n] - m).astype(BF16)
    acc = _dot(vt_ref[0, h, :, 0:nb * MOBA_BLOCK], p_ref[0:nb * MOBA_BLOCK, :])
    acc_ref[h] = acc[0:HEAD_DIM] * (1.0 / acc[HEAD_DIM:HEAD_DIM + 1])


def _moba_kernel(q_ref, k_ref, vt_ref, km_ref, z_ref, h_ref, mod_ref, wout_ref, pong_ref,
                 o_ref, sa_ref, sb_ref, pa_ref, pb_ref, acc_ref, og_ref):
    d = h_ref.shape[-1]
    i = pl.program_id(1)
    nblk = km_ref.shape[2]
    key = lax.broadcasted_iota(jnp.int32, (MOBA_BLOCK, TILE), 0)
    qry = lax.broadcasted_iota(jnp.int32, (MOBA_BLOCK, TILE), 1)
    causal = key <= qry
    blk = lax.broadcasted_iota(jnp.int32, (nblk, TILE), 0)

    for own in range(nblk):
        @pl.when(i == own)
        def _(own=own):
            nb = own + 1
            scores = lambda h, s_ref: _moba_scores(h, nb, q_ref, k_ref, km_ref, s_ref, causal, blk)
            values = lambda h, m, s_ref, p_ref: _moba_values(h, nb, m, vt_ref, s_ref, p_ref, acc_ref)

            def trip(t, m_even):
                h = 2 * t
                m_odd = scores(h + 1, sb_ref)
                values(h, m_even, sa_ref, pa_ref)
                m_even = scores(h + 2, sa_ref)
                values(h + 1, m_odd, sb_ref, pb_ref)
                return m_even

            m_even = lax.fori_loop(0, HEADS // 2 - 1, trip, scores(0, sa_ref))
            m_odd = scores(HEADS - 1, sb_ref)
            values(HEADS - 2, m_even, sa_ref, pa_ref)
            values(HEADS - 1, m_odd, sb_ref, pb_ref)

    for h in range(HEADS):
        hs = slice(h * HEAD_DIM, (h + 1) * HEAD_DIM)
        o = acc_ref[h].T
        og_ref[:, hs] = (o * _silu(z_ref[0, :, hs].astype(F32))).astype(BF16)

    y = _dot(og_ref[...], wout_ref[...])
    gate = mod_ref[0][:, 2 * d:]
    o_ref[0] = h_ref[0] + gate * (_rms(y) * pong_ref[...])


def _moba(q, k, vt, kmean, z, h, mod, w_out, post_g):
    bsz, t, d = h.shape
    nt = t // TILE
    const = lambda *shape: pl.BlockSpec(shape, lambda b, i: (0,) * len(shape))
    tile = pl.BlockSpec((1, TILE, d), lambda b, i: (b, i, 0))
    return pl.pallas_call(
        _moba_kernel,
        grid=(bsz, nt),
        in_specs=[pl.BlockSpec((1, HEADS, TILE, HEAD_DIM), lambda b, i: (b, 0, i, 0)),
                  pl.BlockSpec((1, HEADS, t, HEAD_DIM), lambda b, i: (b, 0, 0, 0)),
                  pl.BlockSpec((1, HEADS, VT_ROWS, t), lambda b, i: (b, 0, 0, 0)),
                  pl.BlockSpec((1, HEADS, nt, HEAD_DIM), lambda b, i: (b, 0, 0, 0)),
                  tile, tile, pl.BlockSpec((1, 1, 3 * d), lambda b, i: (b, 0, 0)),
                  const(d, d), const(1, d)],
        out_specs=tile,
        out_shape=jax.ShapeDtypeStruct((bsz, t, d), F32),
        scratch_shapes=[
            pltpu.VMEM((nt, MOBA_BLOCK, TILE), F32),
            pltpu.VMEM((nt, MOBA_BLOCK, TILE), F32),
            pltpu.VMEM((t, TILE), BF16),
            pltpu.VMEM((t, TILE), BF16),
            pltpu.VMEM((HEADS, HEAD_DIM, TILE), F32),
            pltpu.VMEM((TILE, d), BF16),
        ],
        compiler_params=pltpu.CompilerParams(
            dimension_semantics=("arbitrary", "arbitrary"), vmem_limit_bytes=VMEM_LIMIT),
        name="moba_attention",
    )(q, k, vt, kmean, z, h, mod.reshape(bsz, 1, 3 * d), w_out, post_g.reshape(1, d))


def kernel(x, c, positions, mod_w, mod_b, pre_norm_g, post_norm_g, a_w_in, a_w_out, a_out_norm_g,
           a_lb_logits, kv_norm_g, w_kv, b_w_in, b_w_out):
    assert x.shape[1] % TILE == 0 and x.shape[2] == HEADS * HEAD_DIM
    assert mod_w.shape[0] == 2 and a_w_in.shape[0] == 1 and b_w_in.shape[0] == 1
    mod = _modulation(c, mod_w, mod_b)
    cosf, sinf = _rope_tables(positions)
    h1 = _layer0(x, mod[0], pre_norm_g[0], a_w_in[0].astype(BF16), a_w_out[0].astype(BF16),
                 a_out_norm_g[0], a_lb_logits, post_norm_g[0])
    d = x.shape[2]
    k, vt, q, z, kmean = _proj1(h1, mod[1], kv_norm_g, pre_norm_g[1], w_kv[:, :d].astype(BF16),
                                w_kv[:, d:].T.astype(BF16), b_w_in[0].astype(BF16), cosf, sinf)
    kmean = jnp.transpose(kmean, (0, 2, 1, 3))
    return _moba(q, k, vt, kmean, z, h1, mod[1], b_w_out[0].astype(BF16), post_norm_g[1])
```

```python
import math

import jax
import jax.numpy as jnp
from jax import lax
from jax.experimental import pallas as pl
from jax.experimental.pallas import tpu as pltpu

F32 = jnp.float32
BF16 = jnp.bfloat16

NORM_EPS = 1e-6
ROPE_THETA = 10000.0
HEADS = 8
HEAD_DIM = 128
MOBA_BLOCK = 256
MOBA_TOPK = 3
TILE = 256
BASE = 8
EXP2_CLAMP = 115.0
NEG = -1e30
LOG2E = 1.4426950408889634
VT_ROWS = HEAD_DIM + 16
VMEM_LIMIT = 56 * 1024 * 1024


def _nt(a, b):
    return lax.dot_general(a, b, (((1,), (1,)), ((), ())), preferred_element_type=F32)


def _tn(a, b):
    return lax.dot_general(a, b, (((0,), (0,)), ((), ())), preferred_element_type=F32)


def _dot(a, b):
    return jnp.dot(a, b, preferred_element_type=F32)


def _split(x):
    hi = x.astype(BF16)
    lo = (x - hi.astype(F32)).astype(BF16)
    return hi, lo


def _silu(x):
    return x * (1.0 / (1.0 + jnp.exp(-x)))


def _rms(x):
    return x * lax.rsqrt(jnp.mean(x * x, axis=-1, keepdims=True) + NORM_EPS)


def _mod_kernel(c_ref, w_ref, b_ref, o_ref):
    a = _silu(c_ref[...])
    ah, al = _split(a)
    wh, wl = _split(w_ref[0])
    o_ref[0] = _dot(ah, wh) + (_dot(ah, wl) + _dot(al, wh)) + b_ref[0]


def _modulation(c, mod_w, mod_b):
    depth, d, d3 = mod_w.shape
    bsz = c.shape[0]
    nb = d3 // d
    return pl.pallas_call(
        _mod_kernel,
        grid=(depth, nb),
        in_specs=[
            pl.BlockSpec((bsz, d), lambda l, j: (0, 0)),
            pl.BlockSpec((1, d, d), lambda l, j: (l, 0, j)),
            pl.BlockSpec((1, 1, d), lambda l, j: (l, 0, j)),
        ],
        out_specs=pl.BlockSpec((1, bsz, d), lambda l, j: (l, 0, j)),
        out_shape=jax.ShapeDtypeStruct((depth, bsz, d3), F32),
        compiler_params=pltpu.CompilerParams(vmem_limit_bytes=VMEM_LIMIT),
        name="modulation",
    )(c, mod_w, mod_b.reshape(depth, 1, d3))


def _rope_kernel(pos_ref, cos_ref, sin_ref):
    t2 = pos_ref.shape[1] // 2
    lane = lax.broadcasted_iota(jnp.int32, (1, HEAD_DIM), 1)
    half = HEAD_DIM // 2
    low = lane < half
    i = (lane & (half - 1)).astype(F32)
    inv = jnp.exp(i * (-2.0 / HEAD_DIM * math.log(ROPE_THETA)))
    pos_a = pos_ref[0, 0:t2].astype(F32)
    pos_b = pos_ref[0, t2:2 * t2].astype(F32)
    ang = jnp.where(low, pos_a, pos_b) * inv
    c = jnp.cos(ang)
    s = jnp.sin(ang)
    cr = pltpu.roll(c, half, 1)
    sr = pltpu.roll(s, half, 1)
    cos_ref[0, 0:t2] = jnp.where(low, c, cr)
    cos_ref[0, t2:2 * t2] = jnp.where(low, cr, c)
    sin_ref[0, 0:t2] = jnp.where(low, -s, sr)
    sin_ref[0, t2:2 * t2] = jnp.where(low, -sr, s)


def _rope_tables(positions):
    bsz, t = positions.shape
    out = jax.ShapeDtypeStruct((bsz, t, HEAD_DIM), F32)
    return pl.pallas_call(
        _rope_kernel,
        grid=(bsz,),
        in_specs=[pl.BlockSpec((1, t, 1), lambda b: (b, 0, 0))],
        out_specs=[pl.BlockSpec((1, t, HEAD_DIM), lambda b: (b, 0, 0))] * 2,
        out_shape=[out, out],
        compiler_params=pltpu.CompilerParams(vmem_limit_bytes=VMEM_LIMIT),
        name="rope_tables",
    )(positions.reshape(bsz, t, 1))


def _rope(x, cosf, sinf):
    return x * cosf + pltpu.roll(x, HEAD_DIM // 2, 1) * sinf


def _pair_levels():
    r = lax.broadcasted_iota(jnp.int32, (TILE, TILE), 0)
    c = lax.broadcasted_iota(jnp.int32, (TILE, TILE), 1)
    x = r ^ c
    lvl = jnp.zeros((TILE, TILE), jnp.int32)
    m = BASE
    while m < TILE:
        lvl = lvl + (x >= m).astype(jnp.int32)
        m *= 2
    return jnp.where(c <= r, lvl, -1)


def _hgrn2_head(q, k, v, b, st, lvl):
    nb = TILE // BASE
    b3 = b.reshape(nb, BASE, HEAD_DIM)
    e = (b3 - b3[:, BASE // 2 - 1:BASE // 2, :]).reshape(TILE, HEAD_DIM)
    xq = jnp.exp2(jnp.minimum(e, EXP2_CLAMP))
    xk = jnp.exp2(jnp.minimum(-e, EXP2_CLAMP))
    p = _nt((q * xq).astype(BF16), (k * xk).astype(BF16))
    att = jnp.where(lvl == 0, p, 0.0)
    m = BASE
    level = 1
    while m < TILE:
        nb = TILE // (2 * m)
        b3 = b.reshape(nb, 2 * m, HEAD_DIM)
        e = (b3 - b3[:, m - 1:m, :]).reshape(TILE, HEAD_DIM)
        x = jnp.exp2(-jnp.abs(e))
        p = _nt((q * x).astype(BF16), (k * x).astype(BF16))
        att = att + jnp.where(lvl == level, p, 0.0)
        m *= 2
        level += 1
    o = _dot(att.astype(BF16), v) + _nt((q * jnp.exp2(b)).astype(BF16), st.astype(BF16))
    b_end = b[TILE - 1:TILE, :]
    kw = (k * jnp.exp2(b_end - b)).astype(BF16)
    st_new = st * jnp.exp2(b_end) + _tn(v, kw)
    return o, st_new


def _layer0_kernel(x_ref, mod_ref, png_ref, win_ref, wout_ref, ong_ref, lbl_ref, pong_ref,
                   o_ref, st_ref, og_ref):
    @pl.when(pl.program_id(1) == 0)
    def _():
        st_ref[...] = jnp.zeros_like(st_ref)

    d = x_ref.shape[-1]
    x = x_ref[0]
    mod = mod_ref[0]
    shift, scale, gate = mod[:, :d], mod[:, d:2 * d], mod[:, 2 * d:]
    u = (_rms(x) * png_ref[...]) * (1.0 + scale) + shift
    ub = u.astype(BF16)

    lbl = lbl_ref[...]
    ex = jnp.exp(lbl - jnp.max(lbl, axis=0, keepdims=True))
    lb = ex[0:1, :] / jnp.sum(ex, axis=0, keepdims=True)

    fg = lb + (1.0 - lb) * (1.0 / (1.0 + jnp.exp(-_dot(ub, win_ref[:, d:2 * d]))))
    qa = _silu(_dot(ub, win_ref[:, 0:d]))
    va = _dot(ub, win_ref[:, 2 * d:3 * d]).astype(BF16)
    ga = _silu(_dot(ub, win_ref[:, 3 * d:4 * d]))
    ka = 1.0 - fg
    lf_hi, lf_lo = _split(jnp.log2(fg))
    r = lax.broadcasted_iota(jnp.int32, (TILE, TILE), 0)
    c = lax.broadcasted_iota(jnp.int32, (TILE, TILE), 1)
    tri = (c <= r).astype(BF16)
    bcum = _dot(tri, lf_hi) + _dot(tri, lf_lo)

    lvl = _pair_levels()
    for h in range(HEADS):
        hs = slice(h * HEAD_DIM, (h + 1) * HEAD_DIM)
        o, st_new = _hgrn2_head(qa[:, hs], ka[:, hs], va[:, hs], bcum[:, hs], st_ref[h], lvl)
        st_ref[h] = st_new
        og_ref[:, hs] = ((_rms(o) * ong_ref[...]) * ga[:, hs]).astype(BF16)

    y = _dot(og_ref[...], wout_ref[...])
    o_ref[0] = x + gate * (_rms(y) * pong_ref[...])


def _layer0(x, mod, pre_g, w_in, w_out, out_g, lb_logits, post_g):
    bsz, t, d = x.shape
    nt = t // TILE
    const = lambda *shape: pl.BlockSpec(shape, lambda b, i: (0,) * len(shape))
    return pl.pallas_call(
        _layer0_kernel,
        grid=(bsz, nt),
        in_specs=[
            pl.BlockSpec((1, TILE, d), lambda b, i: (b, i, 0)),
            pl.BlockSpec((1, 1, 3 * d), lambda b, i: (b, 0, 0)),
            const(1, d),
            const(d, 4 * d),
            const(d, d),
            const(1, HEAD_DIM),
            const(lb_logits.shape[0], d),
            const(1, d),
        ],
        out_specs=pl.BlockSpec((1, TILE, d), lambda b, i: (b, i, 0)),
        out_shape=jax.ShapeDtypeStruct((bsz, t, d), F32),
        scratch_shapes=[
            pltpu.VMEM((HEADS, HEAD_DIM, HEAD_DIM), F32),
            pltpu.VMEM((TILE, d), BF16),
        ],
        compiler_params=pltpu.CompilerParams(
            dimension_semantics=("arbitrary", "arbitrary"), vmem_limit_bytes=VMEM_LIMIT),
        name="hgrn2_layer",
    )(x, mod.reshape(bsz, 1, 3 * d), pre_g.reshape(1, d), w_in, w_out,
      out_g.reshape(1, HEAD_DIM), lb_logits, post_g.reshape(1, d))


def _proj1_kernel(h_ref, mod_ref, kvg_ref, png_ref, wk_ref, wvt_ref, win_ref, cos_ref, sin_ref,
                  k_ref, vt_ref, q_ref, z_ref, km_ref):
    d = h_ref.shape[-1]
    xn = _rms(h_ref[0])
    mod = mod_ref[0]
    shift, scale = mod[:, :d], mod[:, d:2 * d]
    a = (xn * kvg_ref[...]).astype(BF16)
    u = ((xn * png_ref[...]) * (1.0 + scale) + shift).astype(BF16)
    cosf, sinf = cos_ref[0], sin_ref[0]
    kk = _dot(a, wk_ref[...])
    qq = _dot(u, win_ref[:, 0:d])
    qscale = HEAD_DIM ** -0.5 * LOG2E
    vt = _nt(wvt_ref[...], a).astype(BF16)
    ones = jnp.ones((VT_ROWS - HEAD_DIM, TILE), BF16)
    for h in range(HEADS):
        hs = slice(h * HEAD_DIM, (h + 1) * HEAD_DIM)
        kr = _rope(kk[:, hs], cosf, sinf)
        k_ref[0, h] = kr.astype(BF16)
        km_ref[0, 0, h:h + 1, :] = jnp.mean(kr, axis=0, keepdims=True)
        q_ref[0, h] = (_rope(qq[:, hs], cosf, sinf) * qscale).astype(BF16)
        vt_ref[0, h, 0:HEAD_DIM, :] = vt[hs, :]
        vt_ref[0, h, HEAD_DIM:VT_ROWS, :] = ones
    zz = _dot(u, win_ref[:, d:2 * d]).astype(BF16)
    for h in range(HEADS):
        z_ref[0, h] = zz[:, h * HEAD_DIM:(h + 1) * HEAD_DIM]


def _proj1(h, mod, kv_g, pre_g, w_k, w_vt, w_in, cosf, sinf):
    bsz, t, d = h.shape
    nt = t // TILE
    const = lambda *shape: pl.BlockSpec(shape, lambda b, i: (0,) * len(shape))
    tile = pl.BlockSpec((1, TILE, d), lambda b, i: (b, i, 0))
    rope = pl.BlockSpec((1, TILE, HEAD_DIM), lambda b, i: (b, i, 0))
    heads = pl.BlockSpec((1, HEADS, TILE, HEAD_DIM), lambda b, i: (b, 0, i, 0))
    heads_shape = jax.ShapeDtypeStruct((bsz, HEADS, t, HEAD_DIM), BF16)
    return pl.pallas_call(
        _proj1_kernel,
        grid=(bsz, nt),
        in_specs=[tile, pl.BlockSpec((1, 1, 3 * d), lambda b, i: (b, 0, 0)),
                  const(1, d), const(1, d), const(d, d), const(d, d), const(d, 2 * d), rope, rope],
        out_specs=[heads, pl.BlockSpec((1, HEADS, VT_ROWS, TILE), lambda b, i: (b, 0, 0, i)), heads, heads,
                   pl.BlockSpec((1, 1, HEADS, HEAD_DIM), lambda b, i: (b, i, 0, 0))],
        out_shape=[heads_shape, jax.ShapeDtypeStruct((bsz, HEADS, VT_ROWS, t), BF16), heads_shape, heads_shape,
                   jax.ShapeDtypeStruct((bsz, nt, HEADS, HEAD_DIM), F32)],
        compiler_params=pltpu.CompilerParams(
            dimension_semantics=("arbitrary", "arbitrary"), vmem_limit_bytes=VMEM_LIMIT),
        name="moba_proj",
    )(h, mod.reshape(bsz, 1, 3 * d), kv_g.reshape(1, d), pre_g.reshape(1, d), w_k, w_vt, w_in, cosf, sinf)


def _moba_scores(h, nb, q_ref, k_ref, km_ref, s_ref, causal, blk):
    npast = nb - 1
    groups = (MOBA_BLOCK // 8, 8, TILE)
    q = q_ref[0, h]
    keys = k_ref[0, h, 0:nb * MOBA_BLOCK, :]
    bias = None
    off = 0
    if npast > MOBA_TOPK:
        km = km_ref[0, h]
        km_hi = km.astype(BF16).astype(F32)
        keys = jnp.concatenate([jnp.concatenate([km_hi, km - km_hi], axis=0).astype(BF16), keys], axis=0)
        off = 2 * km.shape[0]
    s_all = _nt(keys, q)
    if off:
        g = s_all[0:off // 2] + s_all[off // 2:off]
        g = jnp.where(blk < npast, g, -jnp.inf)
        rank = jnp.zeros(g.shape, jnp.int32)
        for m in range(npast):
            gm = g[m:m + 1, :]
            rank = rank + ((gm > g) | ((gm == g) & (m < blk))).astype(jnp.int32)
        bias = jnp.where(rank < MOBA_TOPK, 0.0, NEG)

    m8 = None
    for n in range(nb):
        s = s_all[off + n * MOBA_BLOCK:off + (n + 1) * MOBA_BLOCK]
        if n == npast:
            s = jnp.where(causal, s, NEG)
        elif bias is not None:
            s = s + bias[n:n + 1, :]
        s_ref[n] = s
        mb = jnp.max(s.reshape(groups), axis=0)
        m8 = mb if m8 is None else jnp.maximum(m8, mb)
    return jnp.max(m8, axis=0, keepdims=True)


def _moba_values(h, nb, m, vt_ref, z_ref, s_ref, p_ref, og_ref):
    for n in range(nb):
        p_ref[n * MOBA_BLOCK:(n + 1) * MOBA_BLOCK, :] = jnp.exp2(s_ref[n] - m).astype(BF16)
    acc = _dot(vt_ref[0, h, :, 0:nb * MOBA_BLOCK], p_ref[0:nb * MOBA_BLOCK, :])
    o = (acc[0:HEAD_DIM] * (1.0 / acc[HEAD_DIM:HEAD_DIM + 1])).T
    og_ref[h] = (o * _silu(z_ref[0, h].astype(F32))).astype(BF16)


def _moba_kernel(q_ref, k_ref, vt_ref, km_ref, z_ref, h_ref, mod_ref, wout_ref, pong_ref,
                 o_ref, sa_ref, sb_ref, pa_ref, pb_ref, og_ref):
    d = h_ref.shape[-1]
    i = pl.program_id(1)
    nblk = km_ref.shape[2]
    key = lax.broadcasted_iota(jnp.int32, (MOBA_BLOCK, TILE), 0)
    qry = lax.broadcasted_iota(jnp.int32, (MOBA_BLOCK, TILE), 1)
    causal = key <= qry
    blk = lax.broadcasted_iota(jnp.int32, (nblk, TILE), 0)

    for own in range(nblk):
        @pl.when(i == own)
        def _(own=own):
            nb = own + 1
            scores = lambda h, s_ref: _moba_scores(h, nb, q_ref, k_ref, km_ref, s_ref, causal, blk)
            values = lambda h, m, s_ref, p_ref: _moba_values(h, nb, m, vt_ref, z_ref, s_ref, p_ref, og_ref)

            bufs = ((sa_ref, pa_ref), (sb_ref, pb_ref))
            m = scores(0, sa_ref)
            for h in range(HEADS):
                m_next = scores(h + 1, bufs[(h + 1) % 2][0]) if h + 1 < HEADS else None
                values(h, m, *bufs[h % 2])
                m = m_next

    y = _dot(jnp.concatenate([og_ref[h] for h in range(HEADS)], axis=1), wout_ref[...])
    gate = mod_ref[0][:, 2 * d:]
    o_ref[0] = h_ref[0] + gate * (_rms(y) * pong_ref[...])


def _moba(q, k, vt, kmean, z, h, mod, w_out, post_g):
    bsz, t, d = h.shape
    nt = t // TILE
    const = lambda *shape: pl.BlockSpec(shape, lambda b, i: (0,) * len(shape))
    tile = pl.BlockSpec((1, TILE, d), lambda b, i: (b, i, 0))
    return pl.pallas_call(
        _moba_kernel,
        grid=(bsz, nt),
        in_specs=[pl.BlockSpec((1, HEADS, TILE, HEAD_DIM), lambda b, i: (b, 0, i, 0)),
                  pl.BlockSpec((1, HEADS, t, HEAD_DIM), lambda b, i: (b, 0, 0, 0)),
                  pl.BlockSpec((1, HEADS, VT_ROWS, t), lambda b, i: (b, 0, 0, 0)),
                  pl.BlockSpec((1, HEADS, nt, HEAD_DIM), lambda b, i: (b, 0, 0, 0)),
                  pl.BlockSpec((1, HEADS, TILE, HEAD_DIM), lambda b, i: (b, 0, i, 0)),
                  tile, pl.BlockSpec((1, 1, 3 * d), lambda b, i: (b, 0, 0)),
                  const(d, d), const(1, d)],
        out_specs=tile,
        out_shape=jax.ShapeDtypeStruct((bsz, t, d), F32),
        scratch_shapes=[
            pltpu.VMEM((nt, MOBA_BLOCK, TILE), F32),
            pltpu.VMEM((nt, MOBA_BLOCK, TILE), F32),
            pltpu.VMEM((t, TILE), BF16),
            pltpu.VMEM((t, TILE), BF16),
            pltpu.VMEM((HEADS, TILE, HEAD_DIM), BF16),
        ],
        compiler_params=pltpu.CompilerParams(
            dimension_semantics=("arbitrary", "arbitrary"), vmem_limit_bytes=VMEM_LIMIT),
        name="moba_attention",
    )(q, k, vt, kmean, z, h, mod.reshape(bsz, 1, 3 * d), w_out, post_g.reshape(1, d))


def kernel(x, c, positions, mod_w, mod_b, pre_norm_g, post_norm_g, a_w_in, a_w_out, a_out_norm_g,
           a_lb_logits, kv_norm_g, w_kv, b_w_in, b_w_out):
    assert x.shape[1] % TILE == 0 and x.shape[2] == HEADS * HEAD_DIM
    assert mod_w.shape[0] == 2 and a_w_in.shape[0] == 1 and b_w_in.shape[0] == 1
    mod = _modulation(c, mod_w, mod_b)
    cosf, sinf = _rope_tables(positions)
    h1 = _layer0(x, mod[0], pre_norm_g[0], a_w_in[0].astype(BF16), a_w_out[0].astype(BF16),
                 a_out_norm_g[0], a_lb_logits, post_norm_g[0])
    d = x.shape[2]
    k, vt, q, z, kmean = _proj1(h1, mod[1], kv_norm_g, pre_norm_g[1], w_kv[:, :d].astype(BF16),
                                w_kv[:, d:].T.astype(BF16), b_w_in[0].astype(BF16), cosf, sinf)
    kmean = jnp.transpose(kmean, (0, 2, 1, 3))
    return _moba(q, k, vt, kmean, z, h1, mod[1], b_w_out[0].astype(BF16), post_norm_g[1])
```

```python
import math

import jax
import jax.numpy as jnp
from jax import lax
from jax.experimental import pallas as pl
from jax.experimental.pallas import tpu as pltpu

F32 = jnp.float32
BF16 = jnp.bfloat16

NORM_EPS = 1e-6
ROPE_THETA = 10000.0
HEADS = 8
HEAD_DIM = 128
MOBA_BLOCK = 256
MOBA_TOPK = 3
TILE = 256
BASE = 8
EXP2_CLAMP = 115.0
NEG = -1e30
LOG2E = 1.4426950408889634
VT_ROWS = HEAD_DIM + 16
LOOKAHEAD = 2
VMEM_LIMIT = 56 * 1024 * 1024


def _nt(a, b):
    return lax.dot_general(a, b, (((1,), (1,)), ((), ())), preferred_element_type=F32)


def _tn(a, b):
    return lax.dot_general(a, b, (((0,), (0,)), ((), ())), preferred_element_type=F32)


def _dot(a, b):
    return jnp.dot(a, b, preferred_element_type=F32)


def _split(x):
    hi = x.astype(BF16)
    lo = (x - hi.astype(F32)).astype(BF16)
    return hi, lo


def _silu(x):
    return x * (1.0 / (1.0 + jnp.exp(-x)))


def _rms(x):
    return x * lax.rsqrt(jnp.mean(x * x, axis=-1, keepdims=True) + NORM_EPS)


def _mod_kernel(c_ref, w_ref, b_ref, o_ref):
    a = _silu(c_ref[...])
    ah, al = _split(a)
    wh, wl = _split(w_ref[0])
    o_ref[0] = _dot(ah, wh) + (_dot(ah, wl) + _dot(al, wh)) + b_ref[0]


def _modulation(c, mod_w, mod_b):
    depth, d, d3 = mod_w.shape
    bsz = c.shape[0]
    nb = d3 // d
    return pl.pallas_call(
        _mod_kernel,
        grid=(depth, nb),
        in_specs=[
            pl.BlockSpec((bsz, d), lambda l, j: (0, 0)),
            pl.BlockSpec((1, d, d), lambda l, j: (l, 0, j)),
            pl.BlockSpec((1, 1, d), lambda l, j: (l, 0, j)),
        ],
        out_specs=pl.BlockSpec((1, bsz, d), lambda l, j: (l, 0, j)),
        out_shape=jax.ShapeDtypeStruct((depth, bsz, d3), F32),
        compiler_params=pltpu.CompilerParams(vmem_limit_bytes=VMEM_LIMIT),
        name="modulation",
    )(c, mod_w, mod_b.reshape(depth, 1, d3))


def _rope_kernel(pos_ref, cos_ref, sin_ref):
    t2 = pos_ref.shape[1] // 2
    lane = lax.broadcasted_iota(jnp.int32, (1, HEAD_DIM), 1)
    half = HEAD_DIM // 2
    low = lane < half
    i = (lane & (half - 1)).astype(F32)
    inv = jnp.exp(i * (-2.0 / HEAD_DIM * math.log(ROPE_THETA)))
    pos_a = pos_ref[0, 0:t2].astype(F32)
    pos_b = pos_ref[0, t2:2 * t2].astype(F32)
    ang = jnp.where(low, pos_a, pos_b) * inv
    c = jnp.cos(ang)
    s = jnp.sin(ang)
    cr = pltpu.roll(c, half, 1)
    sr = pltpu.roll(s, half, 1)
    cos_ref[0, 0:t2] = jnp.where(low, c, cr)
    cos_ref[0, t2:2 * t2] = jnp.where(low, cr, c)
    sin_ref[0, 0:t2] = jnp.where(low, -s, sr)
    sin_ref[0, t2:2 * t2] = jnp.where(low, -sr, s)


def _rope_tables(positions):
    bsz, t = positions.shape
    out = jax.ShapeDtypeStruct((bsz, t, HEAD_DIM), F32)
    return pl.pallas_call(
        _rope_kernel,
        grid=(bsz,),
        in_specs=[pl.BlockSpec((1, t, 1), lambda b: (b, 0, 0))],
        out_specs=[pl.BlockSpec((1, t, HEAD_DIM), lambda b: (b, 0, 0))] * 2,
        out_shape=[out, out],
        compiler_params=pltpu.CompilerParams(vmem_limit_bytes=VMEM_LIMIT),
        name="rope_tables",
    )(positions.reshape(bsz, t, 1))


def _rope(x, cosf, sinf):
    return x * cosf + pltpu.roll(x, HEAD_DIM // 2, 1) * sinf


def _base_mask():
    r = lax.broadcasted_iota(jnp.int32, (TILE // 2, TILE // 2), 0)
    c = lax.broadcasted_iota(jnp.int32, (TILE // 2, TILE // 2), 1)
    return ((r ^ c) < BASE) & (c <= r)


def _block_masks():
    r = lax.broadcasted_iota(jnp.int32, (TILE // 2, TILE // 2), 0)
    c = lax.broadcasted_iota(jnp.int32, (TILE // 2, TILE // 2), 1)
    x = r ^ c
    masks = {}
    m = BASE
    while m < TILE // 2:
        masks[m] = x < m
        m *= 2
    return masks


def _hgrn2_scores(q, k, v, b, st, base_mask, block_masks):
    half = TILE // 2
    vb = v.astype(BF16)
    nb = TILE // BASE
    b3 = b.reshape(nb, BASE, HEAD_DIM)
    e = (b3 - b3[:, BASE // 2 - 1:BASE // 2, :]).reshape(TILE, HEAD_DIM)
    xq = jnp.exp2(jnp.minimum(e, EXP2_CLAMP))
    xk = jnp.exp2(jnp.minimum(-e, EXP2_CLAMP))
    qx, kx = (q * xq).astype(BF16), (k * xk).astype(BF16)
    base = [jnp.where(base_mask, _nt(qx[r], kx[r]), 0.0).astype(BF16)
            for r in (slice(0, half), slice(half, TILE))]
    probs = []
    m = BASE
    while m < TILE:
        nb = TILE // (2 * m)
        split = lambda a: a.reshape(nb, 2 * m, HEAD_DIM)
        b3 = split(b)
        ref = b3[:, m - 1:m, :]
        qc = (split(q)[:, m:] * jnp.exp2(b3[:, m:] - ref)).reshape(half, HEAD_DIM).astype(BF16)
        kc = (split(k)[:, :m] * jnp.exp2(ref - b3[:, :m])).reshape(half, HEAD_DIM).astype(BF16)
        p = _nt(qc, kc)
        if m in block_masks:
            p = jnp.where(block_masks[m], p, 0.0)
        probs.append(p.astype(BF16))
        m *= 2
    o_state = _nt((q * jnp.exp2(b)).astype(BF16), st.astype(BF16))
    b_end = b[TILE - 1:TILE, :]
    kw = (k * jnp.exp2(b_end - b)).astype(BF16)
    st_new = st * jnp.exp2(b_end) + _tn(vb, kw)
    return base, probs, o_state, st_new


def _hgrn2_values(base, probs, o_state, v):
    half = TILE // 2
    vb = v.astype(BF16)
    o = o_state + jnp.concatenate([_dot(base[0], vb[0:half]), _dot(base[1], vb[half:TILE])], axis=0)
    rows = [o[r:r + 8] for r in range(0, TILE, 8)]
    m = BASE
    for p in probs:
        nb = TILE // (2 * m)
        vc = v.reshape(nb, 2 * m, HEAD_DIM)[:, :m].reshape(half, HEAD_DIM).astype(BF16)
        oc = _dot(p, vc)
        for j in range(0, half, 8):
            t = (j // m) * 2 * m + m + j % m
            rows[t // 8] = rows[t // 8] + oc[j:j + 8]
        m *= 2
    return jnp.concatenate(rows, axis=0)


def _layer0_kernel(x_ref, mod_ref, png_ref, win_ref, wout_ref, ong_ref, lbl_ref, pong_ref,
                   o_ref, st_ref, og_ref):
    @pl.when(pl.program_id(1) == 0)
    def _():
        st_ref[...] = jnp.zeros_like(st_ref)

    d = x_ref.shape[-1]
    x = x_ref[0]
    mod = mod_ref[0]
    shift, scale, gate = mod[:, :d], mod[:, d:2 * d], mod[:, 2 * d:]
    u = (_rms(x) * png_ref[...]) * (1.0 + scale) + shift
    ub = u.astype(BF16)

    lbl = lbl_ref[...]
    ex = jnp.exp(lbl - jnp.max(lbl, axis=0, keepdims=True))
    lb = ex[0:1, :] / jnp.sum(ex, axis=0, keepdims=True)

    fg = lb + (1.0 - lb) * (1.0 / (1.0 + jnp.exp(-_dot(ub, win_ref[:, d:2 * d]))))
    qa = _silu(_dot(ub, win_ref[:, 0:d]))
    va = _dot(ub, win_ref[:, 2 * d:3 * d])
    ga = _silu(_dot(ub, win_ref[:, 3 * d:4 * d]))
    ka = 1.0 - fg
    lf_hi, lf_lo = _split(jnp.log2(fg))
    r = lax.broadcasted_iota(jnp.int32, (TILE, TILE), 0)
    c = lax.broadcasted_iota(jnp.int32, (TILE, TILE), 1)
    tri = (c <= r).astype(BF16)
    bcum = _dot(tri, lf_hi) + _dot(tri, lf_lo)

    base_mask, block_masks = _base_mask(), _block_masks()
    head = lambda h: slice(h * HEAD_DIM, (h + 1) * HEAD_DIM)

    def scores(h):
        hs = head(h)
        base, probs, o_state, st_new = _hgrn2_scores(qa[:, hs], ka[:, hs], va[:, hs], bcum[:, hs], st_ref[h],
                                                     base_mask, block_masks)
        st_ref[h] = st_new
        return base, probs, o_state

    staged = scores(0)
    for h in range(HEADS):
        staged_next = scores(h + 1) if h + 1 < HEADS else None
        o = _hgrn2_values(*staged, va[:, head(h)])
        og_ref[:, head(h)] = ((_rms(o) * ong_ref[...]) * ga[:, head(h)]).astype(BF16)
        staged = staged_next

    y = _dot(og_ref[...], wout_ref[...])
    o_ref[0] = x + gate * (_rms(y) * pong_ref[...])


def _layer0(x, mod, pre_g, w_in, w_out, out_g, lb_logits, post_g):
    bsz, t, d = x.shape
    nt = t // TILE
    const = lambda *shape: pl.BlockSpec(shape, lambda b, i: (0,) * len(shape))
    return pl.pallas_call(
        _layer0_kernel,
        grid=(bsz, nt),
        in_specs=[
            pl.BlockSpec((1, TILE, d), lambda b, i: (b, i, 0)),
            pl.BlockSpec((1, 1, 3 * d), lambda b, i: (b, 0, 0)),
            const(1, d),
            const(d, 4 * d),
            const(d, d),
            const(1, HEAD_DIM),
            const(lb_logits.shape[0], d),
            const(1, d),
        ],
        out_specs=pl.BlockSpec((1, TILE, d), lambda b, i: (b, i, 0)),
        out_shape=jax.ShapeDtypeStruct((bsz, t, d), F32),
        scratch_shapes=[
            pltpu.VMEM((HEADS, HEAD_DIM, HEAD_DIM), F32),
            pltpu.VMEM((TILE, d), BF16),
        ],
        compiler_params=pltpu.CompilerParams(
            dimension_semantics=("arbitrary", "arbitrary"), vmem_limit_bytes=VMEM_LIMIT),
        name="hgrn2_layer",
    )(x, mod.reshape(bsz, 1, 3 * d), pre_g.reshape(1, d), w_in, w_out,
      out_g.reshape(1, HEAD_DIM), lb_logits, post_g.reshape(1, d))


def _proj1_kernel(h_ref, mod_ref, kvg_ref, png_ref, wk_ref, wvt_ref, win_ref, cos_ref, sin_ref,
                  k_ref, vt_ref, q_ref, z_ref, km_ref):
    d = h_ref.shape[-1]
    xn = _rms(h_ref[0])
    mod = mod_ref[0]
    shift, scale = mod[:, :d], mod[:, d:2 * d]
    a = (xn * kvg_ref[...]).astype(BF16)
    u = ((xn * png_ref[...]) * (1.0 + scale) + shift).astype(BF16)
    cosf, sinf = cos_ref[0], sin_ref[0]
    kk = _dot(a, wk_ref[...])
    qq = _dot(u, win_ref[:, 0:d])
    qscale = HEAD_DIM ** -0.5 * LOG2E
    vt = _nt(wvt_ref[...], a).astype(BF16)
    ones = jnp.ones((VT_ROWS - HEAD_DIM, TILE), BF16)
    for h in range(HEADS):
        hs = slice(h * HEAD_DIM, (h + 1) * HEAD_DIM)
        kr = _rope(kk[:, hs], cosf, sinf)
        k_ref[0, h] = kr.astype(BF16)
        km_ref[0, 0, h:h + 1, :] = jnp.mean(kr, axis=0, keepdims=True)
        q_ref[0, h] = (_rope(qq[:, hs], cosf, sinf) * qscale).astype(BF16)
        vt_ref[0, h, 0:HEAD_DIM, :] = vt[hs, :]
        vt_ref[0, h, HEAD_DIM:VT_ROWS, :] = ones
    zz = _dot(u, win_ref[:, d:2 * d]).astype(BF16)
    for h in range(HEADS):
        z_ref[0, h] = zz[:, h * HEAD_DIM:(h + 1) * HEAD_DIM]


def _proj1(h, mod, kv_g, pre_g, w_k, w_vt, w_in, cosf, sinf):
    bsz, t, d = h.shape
    nt = t // TILE
    const = lambda *shape: pl.BlockSpec(shape, lambda b, i: (0,) * len(shape))
    tile = pl.BlockSpec((1, TILE, d), lambda b, i: (b, i, 0))
    rope = pl.BlockSpec((1, TILE, HEAD_DIM), lambda b, i: (b, i, 0))
    heads = pl.BlockSpec((1, HEADS, TILE, HEAD_DIM), lambda b, i: (b, 0, i, 0))
    heads_shape = jax.ShapeDtypeStruct((bsz, HEADS, t, HEAD_DIM), BF16)
    return pl.pallas_call(
        _proj1_kernel,
        grid=(bsz, nt),
        in_specs=[tile, pl.BlockSpec((1, 1, 3 * d), lambda b, i: (b, 0, 0)),
                  const(1, d), const(1, d), const(d, d), const(d, d), const(d, 2 * d), rope, rope],
        out_specs=[heads, pl.BlockSpec((1, HEADS, VT_ROWS, TILE), lambda b, i: (b, 0, 0, i)), heads, heads,
                   pl.BlockSpec((1, 1, HEADS, HEAD_DIM), lambda b, i: (b, i, 0, 0))],
        out_shape=[heads_shape, jax.ShapeDtypeStruct((bsz, HEADS, VT_ROWS, t), BF16), heads_shape, heads_shape,
                   jax.ShapeDtypeStruct((bsz, nt, HEADS, HEAD_DIM), F32)],
        compiler_params=pltpu.CompilerParams(
            dimension_semantics=("arbitrary", "arbitrary"), vmem_limit_bytes=VMEM_LIMIT),
        name="moba_proj",
    )(h, mod.reshape(bsz, 1, 3 * d), kv_g.reshape(1, d), pre_g.reshape(1, d), w_k, w_vt, w_in, cosf, sinf)


def _moba_scores(h, nb, q_ref, k_ref, km_ref, s_ref, causal, blk):
    npast = nb - 1
    groups = (MOBA_BLOCK // 8, 8, TILE)
    q = q_ref[0, h]
    keys = k_ref[0, h, 0:nb * MOBA_BLOCK, :]
    bias = None
    off = 0
    if npast > MOBA_TOPK:
        km = km_ref[0, h]
        km_hi = km.astype(BF16).astype(F32)
        keys = jnp.concatenate([jnp.concatenate([km_hi, km - km_hi], axis=0).astype(BF16), keys], axis=0)
        off = 2 * km.shape[0]
    s_all = _nt(keys, q)
    if off:
        g = s_all[0:off // 2] + s_all[off // 2:off]
        g = jnp.where(blk < npast, g, -jnp.inf)
        rank = jnp.zeros(g.shape, jnp.int32)
        for m in range(npast):
            gm = g[m:m + 1, :]
            rank = rank + ((gm > g) | ((gm == g) & (m < blk))).astype(jnp.int32)
        bias = jnp.where(rank < MOBA_TOPK, 0.0, NEG)

    m8 = None
    for n in range(nb):
        s = s_all[off + n * MOBA_BLOCK:off + (n + 1) * MOBA_BLOCK]
        if n == npast:
            s = jnp.where(causal, s, NEG)
        elif bias is not None:
            s = s + bias[n:n + 1, :]
        s_ref[n] = s
        mb = jnp.max(s.reshape(groups), axis=0)
        m8 = mb if m8 is None else jnp.maximum(m8, mb)
    return jnp.max(m8, axis=0, keepdims=True)


def _moba_values(h, nb, m, vt_ref, z_ref, s_ref, p_ref, og_ref):
    for n in range(nb):
        p_ref[n * MOBA_BLOCK:(n + 1) * MOBA_BLOCK, :] = jnp.exp2(s_ref[n] - m).astype(BF16)
    acc = _dot(vt_ref[0, h, :, 0:nb * MOBA_BLOCK], p_ref[0:nb * MOBA_BLOCK, :])
    o = (acc[0:HEAD_DIM] * (1.0 / acc[HEAD_DIM:HEAD_DIM + 1])).T
    og_ref[h] = (o * _silu(z_ref[0, h].astype(F32))).astype(BF16)


def _moba_kernel(q_ref, k_ref, vt_ref, km_ref, z_ref, h_ref, mod_ref, wout_ref, pong_ref,
                 o_ref, sa_ref, sb_ref, sc_ref, pa_ref, pb_ref, pc_ref, og_ref):
    d = h_ref.shape[-1]
    i = pl.program_id(1)
    nblk = km_ref.shape[2]
    key = lax.broadcasted_iota(jnp.int32, (MOBA_BLOCK, TILE), 0)
    qry = lax.broadcasted_iota(jnp.int32, (MOBA_BLOCK, TILE), 1)
    causal = key <= qry
    blk = lax.broadcasted_iota(jnp.int32, (nblk, TILE), 0)

    for own in range(nblk):
        @pl.when(i == own)
        def _(own=own):
            nb = own + 1
            scores = lambda h, s_ref: _moba_scores(h, nb, q_ref, k_ref, km_ref, s_ref, causal, blk)
            values = lambda h, m, s_ref, p_ref: _moba_values(h, nb, m, vt_ref, z_ref, s_ref, p_ref, og_ref)

            bufs = ((sa_ref, pa_ref), (sb_ref, pb_ref), (sc_ref, pc_ref))
            maxes = {h: scores(h, bufs[h % 3][0]) for h in range(LOOKAHEAD)}
            for h in range(HEADS):
                if h + LOOKAHEAD < HEADS:
                    maxes[h + LOOKAHEAD] = scores(h + LOOKAHEAD, bufs[(h + LOOKAHEAD) % 3][0])
                values(h, maxes.pop(h), *bufs[h % 3])

    y = _dot(jnp.concatenate([og_ref[h] for h in range(HEADS)], axis=1), wout_ref[...])
    gate = mod_ref[0][:, 2 * d:]
    o_ref[0] = h_ref[0] + gate * (_rms(y) * pong_ref[...])


def _moba(q, k, vt, kmean, z, h, mod, w_out, post_g):
    bsz, t, d = h.shape
    nt = t // TILE
    const = lambda *shape: pl.BlockSpec(shape, lambda b, i: (0,) * len(shape))
    tile = pl.BlockSpec((1, TILE, d), lambda b, i: (b, i, 0))
    return pl.pallas_call(
        _moba_kernel,
        grid=(bsz, nt),
        in_specs=[pl.BlockSpec((1, HEADS, TILE, HEAD_DIM), lambda b, i: (b, 0, i, 0)),
                  pl.BlockSpec((1, HEADS, t, HEAD_DIM), lambda b, i: (b, 0, 0, 0)),
                  pl.BlockSpec((1, HEADS, VT_ROWS, t), lambda b, i: (b, 0, 0, 0)),
                  pl.BlockSpec((1, HEADS, nt, HEAD_DIM), lambda b, i: (b, 0, 0, 0)),
                  pl.BlockSpec((1, HEADS, TILE, HEAD_DIM), lambda b, i: (b, 0, i, 0)),
                  tile, pl.BlockSpec((1, 1, 3 * d), lambda b, i: (b, 0, 0)),
                  const(d, d), const(1, d)],
        out_specs=tile,
        out_shape=jax.ShapeDtypeStruct((bsz, t, d), F32),
        scratch_shapes=[
            pltpu.VMEM((nt, MOBA_BLOCK, TILE), F32),
            pltpu.VMEM((nt, MOBA_BLOCK, TILE), F32),
            pltpu.VMEM((nt, MOBA_BLOCK, TILE), F32),
            pltpu.VMEM((t, TILE), BF16),
            pltpu.VMEM((t, TILE), BF16),
            pltpu.VMEM((t, TILE), BF16),
            pltpu.VMEM((HEADS, TILE, HEAD_DIM), BF16),
        ],
        compiler_params=pltpu.CompilerParams(
            dimension_semantics=("arbitrary", "arbitrary"), vmem_limit_bytes=VMEM_LIMIT),
        name="moba_attention",
    )(q, k, vt, kmean, z, h, mod.reshape(bsz, 1, 3 * d), w_out, post_g.reshape(1, d))


def kernel(x, c, positions, mod_w, mod_b, pre_norm_g, post_norm_g, a_w_in, a_w_out, a_out_norm_g,
           a_lb_logits, kv_norm_g, w_kv, b_w_in, b_w_out):
    assert x.shape[1] % TILE == 0 and x.shape[2] == HEADS * HEAD_DIM
    assert mod_w.shape[0] == 2 and a_w_in.shape[0] == 1 and b_w_in.shape[0] == 1
    mod = _modulation(c, mod_w, mod_b)
    cosf, sinf = _rope_tables(positions)
    h1 = _layer0(x, mod[0], pre_norm_g[0], a_w_in[0].astype(BF16), a_w_out[0].astype(BF16),
                 a_out_norm_g[0], a_lb_logits, post_norm_g[0])
    d = x.shape[2]
    k, vt, q, z, kmean = _proj1(h1, mod[1], kv_norm_g, pre_norm_g[1], w_kv[:, :d].astype(BF16),
                                w_kv[:, d:].T.astype(BF16), b_w_in[0].astype(BF16), cosf, sinf)
    kmean = jnp.transpose(kmean, (0, 2, 1, 3))
    return _moba(q, k, vt, kmean, z, h1, mod[1], b_w_out[0].astype(BF16), post_norm_g[1])
```

```python
import math

import jax
import jax.numpy as jnp
from jax import lax
from jax.experimental import pallas as pl
from jax.experimental.pallas import tpu as pltpu

F32 = jnp.float32
BF16 = jnp.bfloat16

NORM_EPS = 1e-6
ROPE_THETA = 10000.0
HEADS = 8
HEAD_DIM = 128
MOBA_BLOCK = 256
MOBA_TOPK = 3
TILE = 256
STEP_ROWS = 512
BASE = 8
EXP2_CLAMP = 115.0
NEG = -1e30
LOG2E = 1.4426950408889634
VT_ROWS = HEAD_DIM + 16
LOOKAHEAD = 2
VMEM_LIMIT = 56 * 1024 * 1024


def _nt(a, b):
    return lax.dot_general(a, b, (((1,), (1,)), ((), ())), preferred_element_type=F32)


def _tn(a, b):
    return lax.dot_general(a, b, (((0,), (0,)), ((), ())), preferred_element_type=F32)


def _dot(a, b):
    return jnp.dot(a, b, preferred_element_type=F32)


def _split(x):
    hi = x.astype(BF16)
    lo = (x - hi.astype(F32)).astype(BF16)
    return hi, lo


def _sigmoid(x):
    return 1.0 / (1.0 + jnp.exp2(x * (-LOG2E)))


def _silu(x):
    return x * _sigmoid(x)


def _rms(x):
    return x * lax.rsqrt(jnp.mean(x * x, axis=-1, keepdims=True) + NORM_EPS)


def _mod_kernel(c_ref, w_ref, b_ref, o_ref):
    a = _silu(c_ref[...])
    ah, al = _split(a)
    wh, wl = _split(w_ref[0])
    o_ref[0] = _dot(ah, wh) + (_dot(ah, wl) + _dot(al, wh)) + b_ref[0]


def _modulation(c, mod_w, mod_b):
    depth, d, d3 = mod_w.shape
    bsz = c.shape[0]
    nb = d3 // d
    return pl.pallas_call(
        _mod_kernel,
        grid=(depth, nb),
        in_specs=[
            pl.BlockSpec((bsz, d), lambda l, j: (0, 0)),
            pl.BlockSpec((1, d, d), lambda l, j: (l, 0, j)),
            pl.BlockSpec((1, 1, d), lambda l, j: (l, 0, j)),
        ],
        out_specs=pl.BlockSpec((1, bsz, d), lambda l, j: (l, 0, j)),
        out_shape=jax.ShapeDtypeStruct((depth, bsz, d3), F32),
        compiler_params=pltpu.CompilerParams(vmem_limit_bytes=VMEM_LIMIT),
        name="modulation",
    )(c, mod_w, mod_b.reshape(depth, 1, d3))


def _rope_kernel(pos_ref, cos_ref, sin_ref):
    t2 = pos_ref.shape[1] // 2
    lane = lax.broadcasted_iota(jnp.int32, (1, HEAD_DIM), 1)
    half = HEAD_DIM // 2
    low = lane < half
    i = (lane & (half - 1)).astype(F32)
    inv = jnp.exp(i * (-2.0 / HEAD_DIM * math.log(ROPE_THETA)))
    pos_a = pos_ref[0, 0:t2].astype(F32)
    pos_b = pos_ref[0, t2:2 * t2].astype(F32)
    ang = jnp.where(low, pos_a, pos_b) * inv
    c = jnp.cos(ang)
    s = jnp.sin(ang)
    cr = pltpu.roll(c, half, 1)
    sr = pltpu.roll(s, half, 1)
    cos_ref[0, 0:t2] = jnp.where(low, c, cr)
    cos_ref[0, t2:2 * t2] = jnp.where(low, cr, c)
    sin_ref[0, 0:t2] = jnp.where(low, -s, sr)
    sin_ref[0, t2:2 * t2] = jnp.where(low, -sr, s)


def _rope_tables(positions):
    bsz, t = positions.shape
    out = jax.ShapeDtypeStruct((bsz, t, HEAD_DIM), F32)
    return pl.pallas_call(
        _rope_kernel,
        grid=(bsz,),
        in_specs=[pl.BlockSpec((1, t, 1), lambda b: (b, 0, 0))],
        out_specs=[pl.BlockSpec((1, t, HEAD_DIM), lambda b: (b, 0, 0))] * 2,
        out_shape=[out, out],
        compiler_params=pltpu.CompilerParams(vmem_limit_bytes=VMEM_LIMIT),
        name="rope_tables",
    )(positions.reshape(bsz, t, 1))


def _rope(x, cosf, sinf):
    return x * cosf + pltpu.roll(x, HEAD_DIM // 2, 1) * sinf


def _base_mask():
    r = lax.broadcasted_iota(jnp.int32, (TILE // 2, TILE // 2), 0)
    c = lax.broadcasted_iota(jnp.int32, (TILE // 2, TILE // 2), 1)
    return ((r ^ c) < BASE) & (c <= r)


def _block_masks():
    r = lax.broadcasted_iota(jnp.int32, (TILE // 2, TILE // 2), 0)
    c = lax.broadcasted_iota(jnp.int32, (TILE // 2, TILE // 2), 1)
    x = r ^ c
    masks = {}
    m = BASE
    while m < TILE // 2:
        masks[m] = x < m
        m *= 2
    return masks


def _hgrn2_scores(q, k, v, b, st, base_mask, block_masks):
    half = TILE // 2
    vb = v.astype(BF16)
    nb = TILE // BASE
    b3 = b.reshape(nb, BASE, HEAD_DIM)
    e = (b3 - b3[:, BASE // 2 - 1:BASE // 2, :]).reshape(TILE, HEAD_DIM)
    xq = jnp.exp2(jnp.minimum(e, EXP2_CLAMP))
    xk = jnp.exp2(jnp.minimum(-e, EXP2_CLAMP))
    qx, kx = (q * xq).astype(BF16), (k * xk).astype(BF16)
    base = [jnp.where(base_mask, _nt(qx[r], kx[r]), 0.0).astype(BF16)
            for r in (slice(0, half), slice(half, TILE))]
    probs = []
    m = BASE
    while m < TILE:
        nb = TILE // (2 * m)
        split = lambda a: a.reshape(nb, 2 * m, HEAD_DIM)
        b3 = split(b)
        ref = b3[:, m - 1:m, :]
        qc = (split(q)[:, m:] * jnp.exp2(b3[:, m:] - ref)).reshape(half, HEAD_DIM).astype(BF16)
        kc = (split(k)[:, :m] * jnp.exp2(ref - b3[:, :m])).reshape(half, HEAD_DIM).astype(BF16)
        p = _nt(qc, kc)
        if m in block_masks:
            p = jnp.where(block_masks[m], p, 0.0)
        probs.append(p.astype(BF16))
        m *= 2
    o_state = _nt((q * jnp.exp2(b)).astype(BF16), st.astype(BF16))
    b_end = b[TILE - 1:TILE, :]
    kw = (k * jnp.exp2(b_end - b)).astype(BF16)
    st_new = st * jnp.exp2(b_end) + _tn(vb, kw)
    return base, probs, o_state, st_new


def _hgrn2_values(base, probs, o_state, v):
    half = TILE // 2
    vb = v.astype(BF16)
    o = o_state + jnp.concatenate([_dot(base[0], vb[0:half]), _dot(base[1], vb[half:TILE])], axis=0)
    rows = [o[r:r + 8] for r in range(0, TILE, 8)]
    m = BASE
    for p in probs:
        nb = TILE // (2 * m)
        vc = v.reshape(nb, 2 * m, HEAD_DIM)[:, :m].reshape(half, HEAD_DIM).astype(BF16)
        oc = _dot(p, vc)
        for j in range(0, half, 8):
            t = (j // m) * 2 * m + m + j % m
            rows[t // 8] = rows[t // 8] + oc[j:j + 8]
        m *= 2
    return jnp.concatenate(rows, axis=0)


def _layer0_kernel(x_ref, mod_ref, png_ref, win_ref, wout_ref, ong_ref, lbl_ref, pong_ref,
                   o_ref, st_ref, og_ref):
    @pl.when(pl.program_id(1) == 0)
    def _():
        st_ref[...] = jnp.zeros_like(st_ref)

    d = x_ref.shape[-1]
    x = x_ref[0]
    mod = mod_ref[0]
    shift, scale, gate = mod[:, :d], mod[:, d:2 * d], mod[:, 2 * d:]
    u = (_rms(x) * png_ref[...]) * (1.0 + scale) + shift
    ub = u.astype(BF16)

    lbl = lbl_ref[...]
    ex = jnp.exp(lbl - jnp.max(lbl, axis=0, keepdims=True))
    lb = ex[0:1, :] / jnp.sum(ex, axis=0, keepdims=True)

    fg = lb + (1.0 - lb) * _sigmoid(_dot(ub, win_ref[:, d:2 * d]))
    qa = _silu(_dot(ub, win_ref[:, 0:d]))
    va = _dot(ub, win_ref[:, 2 * d:3 * d])
    ga = _silu(_dot(ub, win_ref[:, 3 * d:4 * d]))
    ka = 1.0 - fg
    lf_hi, lf_lo = _split(jnp.log2(fg))
    r = lax.broadcasted_iota(jnp.int32, (TILE, TILE), 0)
    c = lax.broadcasted_iota(jnp.int32, (TILE, TILE), 1)
    tri = (c <= r).astype(BF16)
    chunks = [slice(j * TILE, (j + 1) * TILE) for j in range(x.shape[0] // TILE)]
    bcum = [_dot(tri, lf_hi[rows]) + _dot(tri, lf_lo[rows]) for rows in chunks]

    base_mask, block_masks = _base_mask(), _block_masks()
    head = lambda h: slice(h * HEAD_DIM, (h + 1) * HEAD_DIM)
    units = [(j, h) for j in range(len(chunks)) for h in range(HEADS)]

    def scores(unit):
        j, h = unit
        rows, hs = chunks[j], head(h)
        base, probs, o_state, st_new = _hgrn2_scores(qa[rows, hs], ka[rows, hs], va[rows, hs], bcum[j][:, hs],
                                                     st_ref[h], base_mask, block_masks)
        st_ref[h] = st_new
        return base, probs, o_state

    staged = scores(units[0])
    for n, (j, h) in enumerate(units):
        staged_next = scores(units[n + 1]) if n + 1 < len(units) else None
        rows, hs = chunks[j], head(h)
        o = _hgrn2_values(*staged, va[rows, hs])
        og_ref[rows, hs] = ((_rms(o) * ong_ref[...]) * ga[rows, hs]).astype(BF16)
        staged = staged_next

    y = _dot(og_ref[...], wout_ref[...])
    o_ref[0] = x + gate * (_rms(y) * pong_ref[...])


def _layer0(x, mod, pre_g, w_in, w_out, out_g, lb_logits, post_g):
    bsz, t, d = x.shape
    nt = t // STEP_ROWS
    const = lambda *shape: pl.BlockSpec(shape, lambda b, i: (0,) * len(shape))
    return pl.pallas_call(
        _layer0_kernel,
        grid=(bsz, nt),
        in_specs=[
            pl.BlockSpec((1, STEP_ROWS, d), lambda b, i: (b, i, 0)),
            pl.BlockSpec((1, 1, 3 * d), lambda b, i: (b, 0, 0)),
            const(1, d),
            const(d, 4 * d),
            const(d, d),
            const(1, HEAD_DIM),
            const(lb_logits.shape[0], d),
            const(1, d),
        ],
        out_specs=pl.BlockSpec((1, STEP_ROWS, d), lambda b, i: (b, i, 0)),
        out_shape=jax.ShapeDtypeStruct((bsz, t, d), F32),
        scratch_shapes=[
            pltpu.VMEM((HEADS, HEAD_DIM, HEAD_DIM), F32),
            pltpu.VMEM((STEP_ROWS, d), BF16),
        ],
        compiler_params=pltpu.CompilerParams(
            dimension_semantics=("arbitrary", "arbitrary"), vmem_limit_bytes=VMEM_LIMIT),
        name="hgrn2_layer",
    )(x, mod.reshape(bsz, 1, 3 * d), pre_g.reshape(1, d), w_in, w_out,
      out_g.reshape(1, HEAD_DIM), lb_logits, post_g.reshape(1, d))


def _proj1_kernel(h_ref, mod_ref, kvg_ref, png_ref, wk_ref, wvt_ref, win_ref, cos_ref, sin_ref,
                  k_ref, vt_ref, q_ref, z_ref, km_ref):
    d = h_ref.shape[-1]
    xn = _rms(h_ref[0])
    mod = mod_ref[0]
    shift, scale = mod[:, :d], mod[:, d:2 * d]
    a = (xn * kvg_ref[...]).astype(BF16)
    u = ((xn * png_ref[...]) * (1.0 + scale) + shift).astype(BF16)
    cosf, sinf = cos_ref[0], sin_ref[0]
    kk = _dot(a, wk_ref[...])
    qq = _dot(u, win_ref[:, 0:d])
    qscale = HEAD_DIM ** -0.5 * LOG2E
    vt = _nt(wvt_ref[...], a).astype(BF16)
    rows = h_ref.shape[1]
    ones = jnp.ones((VT_ROWS - HEAD_DIM, rows), BF16)
    for h in range(HEADS):
        hs = slice(h * HEAD_DIM, (h + 1) * HEAD_DIM)
        kr = _rope(kk[:, hs], cosf, sinf)
        k_ref[0, h] = kr.astype(BF16)
        for j in range(rows // MOBA_BLOCK):
            km_ref[0, j, h:h + 1, :] = jnp.mean(kr[j * MOBA_BLOCK:(j + 1) * MOBA_BLOCK], axis=0, keepdims=True)
        q_ref[0, h] = (_rope(qq[:, hs], cosf, sinf) * qscale).astype(BF16)
        vt_ref[0, h, 0:HEAD_DIM, :] = vt[hs, :]
        vt_ref[0, h, HEAD_DIM:VT_ROWS, :] = ones
    zz = _dot(u, win_ref[:, d:2 * d]).astype(BF16)
    for h in range(HEADS):
        z_ref[0, h] = zz[:, h * HEAD_DIM:(h + 1) * HEAD_DIM]


def _proj1(h, mod, kv_g, pre_g, w_k, w_vt, w_in, cosf, sinf):
    bsz, t, d = h.shape
    nt = t // STEP_ROWS
    const = lambda *shape: pl.BlockSpec(shape, lambda b, i: (0,) * len(shape))
    tile = pl.BlockSpec((1, STEP_ROWS, d), lambda b, i: (b, i, 0))
    rope = pl.BlockSpec((1, STEP_ROWS, HEAD_DIM), lambda b, i: (b, i, 0))
    heads = pl.BlockSpec((1, HEADS, STEP_ROWS, HEAD_DIM), lambda b, i: (b, 0, i, 0))
    heads_shape = jax.ShapeDtypeStruct((bsz, HEADS, t, HEAD_DIM), BF16)
    return pl.pallas_call(
        _proj1_kernel,
        grid=(bsz, nt),
        in_specs=[tile, pl.BlockSpec((1, 1, 3 * d), lambda b, i: (b, 0, 0)),
                  const(1, d), const(1, d), const(d, d), const(d, d), const(d, 2 * d), rope, rope],
        out_specs=[heads, pl.BlockSpec((1, HEADS, VT_ROWS, STEP_ROWS), lambda b, i: (b, 0, 0, i)), heads, heads,
                   pl.BlockSpec((1, STEP_ROWS // MOBA_BLOCK, HEADS, HEAD_DIM), lambda b, i: (b, i, 0, 0))],
        out_shape=[heads_shape, jax.ShapeDtypeStruct((bsz, HEADS, VT_ROWS, t), BF16), heads_shape, heads_shape,
                   jax.ShapeDtypeStruct((bsz, t // MOBA_BLOCK, HEADS, HEAD_DIM), F32)],
        compiler_params=pltpu.CompilerParams(
            dimension_semantics=("arbitrary", "arbitrary"), vmem_limit_bytes=VMEM_LIMIT),
        name="moba_proj",
    )(h, mod.reshape(bsz, 1, 3 * d), kv_g.reshape(1, d), pre_g.reshape(1, d), w_k, w_vt, w_in, cosf, sinf)


def _moba_scores(h, nb, q_ref, k_ref, km_ref, s_ref, causal, blk):
    npast = nb - 1
    groups = (MOBA_BLOCK // 8, 8, TILE)
    q = q_ref[0, h]
    keys = k_ref[0, h, 0:nb * MOBA_BLOCK, :]
    bias = None
    off = 0
    if npast > MOBA_TOPK:
        km = km_ref[0, h]
        km_hi = km.astype(BF16).astype(F32)
        keys = jnp.concatenate([jnp.concatenate([km_hi, km - km_hi], axis=0).astype(BF16), keys], axis=0)
        off = 2 * km.shape[0]
    s_all = _nt(keys, q)
    if off:
        g = s_all[0:off // 2] + s_all[off // 2:off]
        g = jnp.where(blk < npast, g, -jnp.inf)
        rank = jnp.zeros(g.shape, jnp.int32)
        for m in range(npast):
            gm = g[m:m + 1, :]
            rank = rank + ((gm > g) | ((gm == g) & (m < blk))).astype(jnp.int32)
        bias = jnp.where(rank < MOBA_TOPK, 0.0, NEG)

    m8 = None
    for n in range(nb):
        s = s_all[off + n * MOBA_BLOCK:off + (n + 1) * MOBA_BLOCK]
        if n == npast:
            s = jnp.where(causal, s, NEG)
        elif bias is not None:
            s = s + bias[n:n + 1, :]
        s_ref[n] = s
        mb = jnp.max(s.reshape(groups), axis=0)
        m8 = mb if m8 is None else jnp.maximum(m8, mb)
    return jnp.max(m8, axis=0, keepdims=True)


def _moba_values(h, nb, m, vt_ref, z_ref, s_ref, p_ref, og_ref):
    for n in range(nb):
        p_ref[n * MOBA_BLOCK:(n + 1) * MOBA_BLOCK, :] = jnp.exp2(s_ref[n] - m).astype(BF16)
    acc = _dot(vt_ref[0, h, :, 0:nb * MOBA_BLOCK], p_ref[0:nb * MOBA_BLOCK, :])
    o = (acc[0:HEAD_DIM] * (1.0 / acc[HEAD_DIM:HEAD_DIM + 1])).T
    og_ref[h] = (o * _silu(z_ref[0, h].astype(F32))).astype(BF16)


def _moba_kernel(q_ref, k_ref, vt_ref, km_ref, z_ref, h_ref, mod_ref, wout_ref, pong_ref,
                 o_ref, sa_ref, sb_ref, sc_ref, pa_ref, pb_ref, pc_ref, og_ref):
    d = h_ref.shape[-1]
    i = pl.program_id(1)
    nblk = km_ref.shape[2]
    key = lax.broadcasted_iota(jnp.int32, (MOBA_BLOCK, TILE), 0)
    qry = lax.broadcasted_iota(jnp.int32, (MOBA_BLOCK, TILE), 1)
    causal = key <= qry
    blk = lax.broadcasted_iota(jnp.int32, (nblk, TILE), 0)

    for own in range(nblk):
        @pl.when(i == own)
        def _(own=own):
            nb = own + 1
            scores = lambda h, s_ref: _moba_scores(h, nb, q_ref, k_ref, km_ref, s_ref, causal, blk)
            values = lambda h, m, s_ref, p_ref: _moba_values(h, nb, m, vt_ref, z_ref, s_ref, p_ref, og_ref)

            bufs = ((sa_ref, pa_ref), (sb_ref, pb_ref), (sc_ref, pc_ref))
            maxes = {h: scores(h, bufs[h % 3][0]) for h in range(LOOKAHEAD)}
            for h in range(HEADS):
                if h + LOOKAHEAD < HEADS:
                    maxes[h + LOOKAHEAD] = scores(h + LOOKAHEAD, bufs[(h + LOOKAHEAD) % 3][0])
                values(h, maxes.pop(h), *bufs[h % 3])

    y = _dot(jnp.concatenate([og_ref[h] for h in range(HEADS)], axis=1), wout_ref[...])
    gate = mod_ref[0][:, 2 * d:]
    o_ref[0] = h_ref[0] + gate * (_rms(y) * pong_ref[...])


def _moba(q, k, vt, kmean, z, h, mod, w_out, post_g):
    bsz, t, d = h.shape
    nt = t // TILE
    const = lambda *shape: pl.BlockSpec(shape, lambda b, i: (0,) * len(shape))
    tile = pl.BlockSpec((1, TILE, d), lambda b, i: (b, i, 0))
    return pl.pallas_call(
        _moba_kernel,
        grid=(bsz, nt),
        in_specs=[pl.BlockSpec((1, HEADS, TILE, HEAD_DIM), lambda b, i: (b, 0, i, 0)),
                  pl.BlockSpec((1, HEADS, t, HEAD_DIM), lambda b, i: (b, 0, 0, 0)),
                  pl.BlockSpec((1, HEADS, VT_ROWS, t), lambda b, i: (b, 0, 0, 0)),
                  pl.BlockSpec((1, HEADS, nt, HEAD_DIM), lambda b, i: (b, 0, 0, 0)),
                  pl.BlockSpec((1, HEADS, TILE, HEAD_DIM), lambda b, i: (b, 0, i, 0)),
                  tile, pl.BlockSpec((1, 1, 3 * d), lambda b, i: (b, 0, 0)),
                  const(d, d), const(1, d)],
        out_specs=tile,
        out_shape=jax.ShapeDtypeStruct((bsz, t, d), F32),
        scratch_shapes=[
            pltpu.VMEM((nt, MOBA_BLOCK, TILE), F32),
            pltpu.VMEM((nt, MOBA_BLOCK, TILE), F32),
            pltpu.VMEM((nt, MOBA_BLOCK, TILE), F32),
            pltpu.VMEM((t, TILE), BF16),
            pltpu.VMEM((t, TILE), BF16),
            pltpu.VMEM((t, TILE), BF16),
            pltpu.VMEM((HEADS, TILE, HEAD_DIM), BF16),
        ],
        compiler_params=pltpu.CompilerParams(
            dimension_semantics=("arbitrary", "arbitrary"), vmem_limit_bytes=VMEM_LIMIT),
        name="moba_attention",
    )(q, k, vt, kmean, z, h, mod.reshape(bsz, 1, 3 * d), w_out, post_g.reshape(1, d))


def kernel(x, c, positions, mod_w, mod_b, pre_norm_g, post_norm_g, a_w_in, a_w_out, a_out_norm_g,
           a_lb_logits, kv_norm_g, w_kv, b_w_in, b_w_out):
    assert x.shape[1] % STEP_ROWS == 0 and STEP_ROWS % TILE == 0 and x.shape[2] == HEADS * HEAD_DIM
    assert mod_w.shape[0] == 2 and a_w_in.shape[0] == 1 and b_w_in.shape[0] == 1
    mod = _modulation(c, mod_w, mod_b)
    cosf, sinf = _rope_tables(positions)
    h1 = _layer0(x, mod[0], pre_norm_g[0], a_w_in[0].astype(BF16), a_w_out[0].astype(BF16),
                 a_out_norm_g[0], a_lb_logits, post_norm_g[0])
    d = x.shape[2]
    k, vt, q, z, kmean = _proj1(h1, mod[1], kv_norm_g, pre_norm_g[1], w_kv[:, :d].astype(BF16),
                                w_kv[:, d:].T.astype(BF16), b_w_in[0].astype(BF16), cosf, sinf)
    kmean = jnp.transpose(kmean, (0, 2, 1, 3))
    return _moba(q, k, vt, kmean, z, h1, mod[1], b_w_out[0].astype(BF16), post_norm_g[1])
```

```python
import math

import jax
import jax.numpy as jnp
from jax import lax
from jax.experimental import pallas as pl
from jax.experimental.pallas import tpu as pltpu

F32 = jnp.float32
BF16 = jnp.bfloat16

NORM_EPS = 1e-6
ROPE_THETA = 10000.0
HEADS = 8
HEAD_DIM = 128
MOBA_BLOCK = 256
MOBA_TOPK = 3
TILE = 256
STEP_ROWS = 512
BASE = 8
EXP2_CLAMP = 115.0
NEG = -1e30
LOG2E = 1.4426950408889634
VT_ROWS = HEAD_DIM + 16
LOOKAHEAD = 2
VMEM_LIMIT = 56 * 1024 * 1024


def _nt(a, b):
    return lax.dot_general(a, b, (((1,), (1,)), ((), ())), preferred_element_type=F32)


def _tn(a, b):
    return lax.dot_general(a, b, (((0,), (0,)), ((), ())), preferred_element_type=F32)


def _dot(a, b):
    return jnp.dot(a, b, preferred_element_type=F32)


def _split(x):
    hi = x.astype(BF16)
    lo = (x - hi.astype(F32)).astype(BF16)
    return hi, lo


def _sigmoid(x):
    return 1.0 / (1.0 + jnp.exp2(x * (-LOG2E)))


def _silu(x):
    return x * _sigmoid(x)


def _rms(x):
    return x * lax.rsqrt(jnp.mean(x * x, axis=-1, keepdims=True) + NORM_EPS)


def _mod_kernel(c_ref, w_ref, b_ref, o_ref):
    a = _silu(c_ref[...])
    ah, al = _split(a)
    wh, wl = _split(w_ref[0])
    o_ref[0] = _dot(ah, wh) + (_dot(ah, wl) + _dot(al, wh)) + b_ref[0]


def _modulation(c, mod_w, mod_b):
    depth, d, d3 = mod_w.shape
    bsz = c.shape[0]
    nb = d3 // d
    return pl.pallas_call(
        _mod_kernel,
        grid=(depth, nb),
        in_specs=[
            pl.BlockSpec((bsz, d), lambda l, j: (0, 0)),
            pl.BlockSpec((1, d, d), lambda l, j: (l, 0, j)),
            pl.BlockSpec((1, 1, d), lambda l, j: (l, 0, j)),
        ],
        out_specs=pl.BlockSpec((1, bsz, d), lambda l, j: (l, 0, j)),
        out_shape=jax.ShapeDtypeStruct((depth, bsz, d3), F32),
        compiler_params=pltpu.CompilerParams(vmem_limit_bytes=VMEM_LIMIT),
        name="modulation",
    )(c, mod_w, mod_b.reshape(depth, 1, d3))


def _rope_kernel(pos_ref, cos_ref, sin_ref):
    t2 = pos_ref.shape[1] // 2
    lane = lax.broadcasted_iota(jnp.int32, (1, HEAD_DIM), 1)
    half = HEAD_DIM // 2
    low = lane < half
    i = (lane & (half - 1)).astype(F32)
    inv = jnp.exp(i * (-2.0 / HEAD_DIM * math.log(ROPE_THETA)))
    pos_a = pos_ref[0, 0:t2].astype(F32)
    pos_b = pos_ref[0, t2:2 * t2].astype(F32)
    ang = jnp.where(low, pos_a, pos_b) * inv
    c = jnp.cos(ang)
    s = jnp.sin(ang)
    cr = pltpu.roll(c, half, 1)
    sr = pltpu.roll(s, half, 1)
    cos_ref[0, 0:t2] = jnp.where(low, c, cr)
    cos_ref[0, t2:2 * t2] = jnp.where(low, cr, c)
    sin_ref[0, 0:t2] = jnp.where(low, -s, sr)
    sin_ref[0, t2:2 * t2] = jnp.where(low, -sr, s)


def _rope_tables(positions):
    bsz, t = positions.shape
    out = jax.ShapeDtypeStruct((bsz, t, HEAD_DIM), F32)
    return pl.pallas_call(
        _rope_kernel,
        grid=(bsz,),
        in_specs=[pl.BlockSpec((1, t, 1), lambda b: (b, 0, 0))],
        out_specs=[pl.BlockSpec((1, t, HEAD_DIM), lambda b: (b, 0, 0))] * 2,
        out_shape=[out, out],
        compiler_params=pltpu.CompilerParams(vmem_limit_bytes=VMEM_LIMIT),
        name="rope_tables",
    )(positions.reshape(bsz, t, 1))


def _rope(x, cosf, sinf):
    return x * cosf + pltpu.roll(x, HEAD_DIM // 2, 1) * sinf


def _base_mask():
    r = lax.broadcasted_iota(jnp.int32, (TILE // 2, TILE // 2), 0)
    c = lax.broadcasted_iota(jnp.int32, (TILE // 2, TILE // 2), 1)
    return ((r ^ c) < BASE) & (c <= r)


def _block_masks():
    r = lax.broadcasted_iota(jnp.int32, (TILE // 2, TILE // 2), 0)
    c = lax.broadcasted_iota(jnp.int32, (TILE // 2, TILE // 2), 1)
    x = r ^ c
    masks = {}
    m = BASE
    while m < TILE // 2:
        masks[m] = x < m
        m *= 2
    return masks


def _hgrn2_scores(q, k, v, b, st, base_mask, block_masks):
    half = TILE // 2
    vb = v.astype(BF16)
    nb = TILE // BASE
    b3 = b.reshape(nb, BASE, HEAD_DIM)
    e = (b3 - b3[:, BASE // 2 - 1:BASE // 2, :]).reshape(TILE, HEAD_DIM)
    xq = jnp.exp2(jnp.minimum(e, EXP2_CLAMP))
    xk = jnp.exp2(jnp.minimum(-e, EXP2_CLAMP))
    qx, kx = (q * xq).astype(BF16), (k * xk).astype(BF16)
    base = [jnp.where(base_mask, _nt(qx[r], kx[r]), 0.0).astype(BF16)
            for r in (slice(0, half), slice(half, TILE))]
    probs = []
    m = BASE
    while m < TILE:
        nb = TILE // (2 * m)
        split = lambda a: a.reshape(nb, 2 * m, HEAD_DIM)
        b3 = split(b)
        ref = b3[:, m - 1:m, :]
        qc = (split(q)[:, m:] * jnp.exp2(b3[:, m:] - ref)).reshape(half, HEAD_DIM).astype(BF16)
        kc = (split(k)[:, :m] * jnp.exp2(ref - b3[:, :m])).reshape(half, HEAD_DIM).astype(BF16)
        p = _nt(qc, kc)
        if m in block_masks:
            p = jnp.where(block_masks[m], p, 0.0)
        probs.append(p.astype(BF16))
        m *= 2
    o_state = _nt((q * jnp.exp2(b)).astype(BF16), st.astype(BF16))
    b_end = b[TILE - 1:TILE, :]
    kw = (k * jnp.exp2(b_end - b)).astype(BF16)
    st_new = st * jnp.exp2(b_end) + _tn(vb, kw)
    return base, probs, o_state, st_new


def _hgrn2_values(base, probs, o_state, v):
    half = TILE // 2
    vb = v.astype(BF16)
    o = o_state + jnp.concatenate([_dot(base[0], vb[0:half]), _dot(base[1], vb[half:TILE])], axis=0)
    rows = [o[r:r + 8] for r in range(0, TILE, 8)]
    m = BASE
    for p in probs:
        nb = TILE // (2 * m)
        vc = v.reshape(nb, 2 * m, HEAD_DIM)[:, :m].reshape(half, HEAD_DIM).astype(BF16)
        oc = _dot(p, vc)
        for j in range(0, half, 8):
            t = (j // m) * 2 * m + m + j % m
            rows[t // 8] = rows[t // 8] + oc[j:j + 8]
        m *= 2
    return jnp.concatenate(rows, axis=0)


def _layer0_kernel(x_ref, mod_ref, png_ref, win_ref, wout_ref, ong_ref, lbl_ref, pong_ref,
                   o_ref, st_ref, og_ref):
    @pl.when(pl.program_id(1) == 0)
    def _():
        st_ref[...] = jnp.zeros_like(st_ref)

    d = x_ref.shape[-1]
    x = x_ref[0]
    mod = mod_ref[0]
    shift, scale, gate = mod[:, :d], mod[:, d:2 * d], mod[:, 2 * d:]
    u = (_rms(x) * png_ref[...]) * (1.0 + scale) + shift
    ub = u.astype(BF16)

    lbl = lbl_ref[...]
    ex = jnp.exp(lbl - jnp.max(lbl, axis=0, keepdims=True))
    lb = ex[0:1, :] / jnp.sum(ex, axis=0, keepdims=True)

    fg = lb + (1.0 - lb) * _sigmoid(_dot(ub, win_ref[:, d:2 * d]))
    qa = _silu(_dot(ub, win_ref[:, 0:d]))
    va = _dot(ub, win_ref[:, 2 * d:3 * d])
    ga = _silu(_dot(ub, win_ref[:, 3 * d:4 * d]))
    ka = 1.0 - fg
    lf_hi, lf_lo = _split(jnp.log2(fg))
    r = lax.broadcasted_iota(jnp.int32, (TILE, TILE), 0)
    c = lax.broadcasted_iota(jnp.int32, (TILE, TILE), 1)
    tri = (c <= r).astype(BF16)
    chunks = [slice(j * TILE, (j + 1) * TILE) for j in range(x.shape[0] // TILE)]
    bcum = [_dot(tri, lf_hi[rows]) + _dot(tri, lf_lo[rows]) for rows in chunks]

    base_mask, block_masks = _base_mask(), _block_masks()
    head = lambda h: slice(h * HEAD_DIM, (h + 1) * HEAD_DIM)
    units = [(j, h) for j in range(len(chunks)) for h in range(HEADS)]

    def scores(unit):
        j, h = unit
        rows, hs = chunks[j], head(h)
        base, probs, o_state, st_new = _hgrn2_scores(qa[rows, hs], ka[rows, hs], va[rows, hs], bcum[j][:, hs],
                                                     st_ref[h], base_mask, block_masks)
        st_ref[h] = st_new
        return base, probs, o_state

    staged = scores(units[0])
    for n, (j, h) in enumerate(units):
        staged_next = scores(units[n + 1]) if n + 1 < len(units) else None
        rows, hs = chunks[j], head(h)
        o = _hgrn2_values(*staged, va[rows, hs])
        og_ref[rows, hs] = ((_rms(o) * ong_ref[...]) * ga[rows, hs]).astype(BF16)
        staged = staged_next

    y = _dot(og_ref[...], wout_ref[...])
    o_ref[0] = x + gate * (_rms(y) * pong_ref[...])


def _layer0(x, mod, pre_g, w_in, w_out, out_g, lb_logits, post_g):
    bsz, t, d = x.shape
    nt = t // STEP_ROWS
    const = lambda *shape: pl.BlockSpec(shape, lambda b, i: (0,) * len(shape))
    return pl.pallas_call(
        _layer0_kernel,
        grid=(bsz, nt),
        in_specs=[
            pl.BlockSpec((1, STEP_ROWS, d), lambda b, i: (b, i, 0)),
            pl.BlockSpec((1, 1, 3 * d), lambda b, i: (b, 0, 0)),
            const(1, d),
            const(d, 4 * d),
            const(d, d),
            const(1, HEAD_DIM),
            const(lb_logits.shape[0], d),
            const(1, d),
        ],
        out_specs=pl.BlockSpec((1, STEP_ROWS, d), lambda b, i: (b, i, 0)),
        out_shape=jax.ShapeDtypeStruct((bsz, t, d), F32),
        scratch_shapes=[
            pltpu.VMEM((HEADS, HEAD_DIM, HEAD_DIM), F32),
            pltpu.VMEM((STEP_ROWS, d), BF16),
        ],
        compiler_params=pltpu.CompilerParams(
            dimension_semantics=("arbitrary", "arbitrary"), vmem_limit_bytes=VMEM_LIMIT),
        name="hgrn2_layer",
    )(x, mod.reshape(bsz, 1, 3 * d), pre_g.reshape(1, d), w_in, w_out,
      out_g.reshape(1, HEAD_DIM), lb_logits, post_g.reshape(1, d))


def _proj1_kernel(h_ref, mod_ref, kvg_ref, png_ref, wk_ref, wvt_ref, win_ref, cos_ref, sin_ref,
                  k_ref, vt_ref, q_ref, z_ref, km_ref):
    d = h_ref.shape[-1]
    xn = _rms(h_ref[0])
    mod = mod_ref[0]
    shift, scale = mod[:, :d], mod[:, d:2 * d]
    a = (xn * kvg_ref[...]).astype(BF16)
    u = ((xn * png_ref[...]) * (1.0 + scale) + shift).astype(BF16)
    cosf, sinf = cos_ref[0], sin_ref[0]
    kk = _dot(a, wk_ref[...])
    qq = _dot(u, win_ref[:, 0:d])
    qscale = HEAD_DIM ** -0.5 * LOG2E
    vt = _nt(wvt_ref[...], a).astype(BF16)
    rows = h_ref.shape[1]
    ones = jnp.ones((VT_ROWS - HEAD_DIM, rows), BF16)
    for h in range(HEADS):
        hs = slice(h * HEAD_DIM, (h + 1) * HEAD_DIM)
        kr = _rope(kk[:, hs], cosf, sinf)
        k_ref[0, h] = kr.astype(BF16)
        for j in range(rows // MOBA_BLOCK):
            km_ref[0, j, h:h + 1, :] = jnp.mean(kr[j * MOBA_BLOCK:(j + 1) * MOBA_BLOCK], axis=0, keepdims=True)
        q_ref[0, h] = (_rope(qq[:, hs], cosf, sinf) * qscale).astype(BF16)
        vt_ref[0, h, 0:HEAD_DIM, :] = vt[hs, :]
        vt_ref[0, h, HEAD_DIM:VT_ROWS, :] = ones
    zz = _dot(u, win_ref[:, d:2 * d]).astype(BF16)
    for h in range(HEADS):
        z_ref[0, h] = zz[:, h * HEAD_DIM:(h + 1) * HEAD_DIM]


def _proj1(h, mod, kv_g, pre_g, w_k, w_vt, w_in, cosf, sinf):
    bsz, t, d = h.shape
    nt = t // STEP_ROWS
    const = lambda *shape: pl.BlockSpec(shape, lambda b, i: (0,) * len(shape))
    tile = pl.BlockSpec((1, STEP_ROWS, d), lambda b, i: (b, i, 0))
    rope = pl.BlockSpec((1, STEP_ROWS, HEAD_DIM), lambda b, i: (b, i, 0))
    heads = pl.BlockSpec((1, HEADS, STEP_ROWS, HEAD_DIM), lambda b, i: (b, 0, i, 0))
    heads_shape = jax.ShapeDtypeStruct((bsz, HEADS, t, HEAD_DIM), BF16)
    return pl.pallas_call(
        _proj1_kernel,
        grid=(bsz, nt),
        in_specs=[tile, pl.BlockSpec((1, 1, 3 * d), lambda b, i: (b, 0, 0)),
                  const(1, d), const(1, d), const(d, d), const(d, d), const(d, 2 * d), rope, rope],
        out_specs=[heads, pl.BlockSpec((1, HEADS, VT_ROWS, STEP_ROWS), lambda b, i: (b, 0, 0, i)), heads, heads,
                   pl.BlockSpec((1, STEP_ROWS // MOBA_BLOCK, HEADS, HEAD_DIM), lambda b, i: (b, i, 0, 0))],
        out_shape=[heads_shape, jax.ShapeDtypeStruct((bsz, HEADS, VT_ROWS, t), BF16), heads_shape, heads_shape,
                   jax.ShapeDtypeStruct((bsz, t // MOBA_BLOCK, HEADS, HEAD_DIM), F32)],
        compiler_params=pltpu.CompilerParams(
            dimension_semantics=("arbitrary", "arbitrary"), vmem_limit_bytes=VMEM_LIMIT),
        name="moba_proj",
    )(h, mod.reshape(bsz, 1, 3 * d), kv_g.reshape(1, d), pre_g.reshape(1, d), w_k, w_vt, w_in, cosf, sinf)


def _moba_scores(h, rows, nb, q_ref, k_ref, km_ref, s_ref, causal, blk):
    npast = nb - 1
    groups = (MOBA_BLOCK // 8, 8, TILE)
    q = q_ref[0, h, rows, :]
    keys = k_ref[0, h, 0:nb * MOBA_BLOCK, :]
    bias = None
    off = 0
    if npast > MOBA_TOPK:
        km = km_ref[0, h]
        km_hi = km.astype(BF16).astype(F32)
        keys = jnp.concatenate([jnp.concatenate([km_hi, km - km_hi], axis=0).astype(BF16), keys], axis=0)
        off = 2 * km.shape[0]
    s_all = _nt(keys, q)
    if off:
        g = s_all[0:off // 2] + s_all[off // 2:off]
        g = jnp.where(blk < npast, g, -jnp.inf)
        rank = jnp.zeros(g.shape, jnp.int32)
        for m in range(npast):
            gm = g[m:m + 1, :]
            rank = rank + ((gm > g) | ((gm == g) & (m < blk))).astype(jnp.int32)
        bias = jnp.where(rank < MOBA_TOPK, 0.0, NEG)

    m8 = None
    for n in range(nb):
        s = s_all[off + n * MOBA_BLOCK:off + (n + 1) * MOBA_BLOCK]
        if n == npast:
            s = jnp.where(causal, s, NEG)
        elif bias is not None:
            s = s + bias[n:n + 1, :]
        s_ref[n] = s
        mb = jnp.max(s.reshape(groups), axis=0)
        m8 = mb if m8 is None else jnp.maximum(m8, mb)
    return jnp.max(m8, axis=0, keepdims=True)


def _moba_values(h, rows, nb, m, vt_ref, z_ref, s_ref, p_ref, og_ref):
    for n in range(nb):
        p_ref[n * MOBA_BLOCK:(n + 1) * MOBA_BLOCK, :] = jnp.exp2(s_ref[n] - m).astype(BF16)
    acc = _dot(vt_ref[0, h, :, 0:nb * MOBA_BLOCK], p_ref[0:nb * MOBA_BLOCK, :])
    o = (acc[0:HEAD_DIM] * (1.0 / acc[HEAD_DIM:HEAD_DIM + 1])).T
    og_ref[h, rows, :] = (o * _silu(z_ref[0, h, rows, :].astype(F32))).astype(BF16)


def _moba_kernel(q_ref, k_ref, vt_ref, km_ref, z_ref, h_ref, mod_ref, wout_ref, pong_ref,
                 o_ref, sa_ref, sb_ref, sc_ref, pa_ref, pb_ref, pc_ref, og_ref):
    d = h_ref.shape[-1]
    i = pl.program_id(1)
    nblk = km_ref.shape[2]
    tiles = h_ref.shape[1] // TILE
    key = lax.broadcasted_iota(jnp.int32, (MOBA_BLOCK, TILE), 0)
    qry = lax.broadcasted_iota(jnp.int32, (MOBA_BLOCK, TILE), 1)
    causal = key <= qry
    blk = lax.broadcasted_iota(jnp.int32, (nblk, TILE), 0)

    for step in range(nblk // tiles):
        @pl.when(i == step)
        def _(step=step):
            units = [(t, h) for t in range(tiles) for h in range(HEADS)]

            def scores(n):
                t, h = units[n]
                return _moba_scores(h, slice(t * TILE, (t + 1) * TILE), step * tiles + t + 1, q_ref, k_ref, km_ref,
                                    bufs[n % 3][0], causal, blk)

            def values(n, m):
                t, h = units[n]
                _moba_values(h, slice(t * TILE, (t + 1) * TILE), step * tiles + t + 1, m, vt_ref, z_ref,
                             *bufs[n % 3], og_ref)

            bufs = ((sa_ref, pa_ref), (sb_ref, pb_ref), (sc_ref, pc_ref))
            maxes = {n: scores(n) for n in range(LOOKAHEAD)}
            for n in range(len(units)):
                if n + LOOKAHEAD < len(units):
                    maxes[n + LOOKAHEAD] = scores(n + LOOKAHEAD)
                values(n, maxes.pop(n))

    y = _dot(jnp.concatenate([og_ref[h] for h in range(HEADS)], axis=1), wout_ref[...])
    gate = mod_ref[0][:, 2 * d:]
    o_ref[0] = h_ref[0] + gate * (_rms(y) * pong_ref[...])


def _moba(q, k, vt, kmean, z, h, mod, w_out, post_g):
    bsz, t, d = h.shape
    nt = t // TILE
    const = lambda *shape: pl.BlockSpec(shape, lambda b, i: (0,) * len(shape))
    tile = pl.BlockSpec((1, STEP_ROWS, d), lambda b, i: (b, i, 0))
    return pl.pallas_call(
        _moba_kernel,
        grid=(bsz, t // STEP_ROWS),
        in_specs=[pl.BlockSpec((1, HEADS, STEP_ROWS, HEAD_DIM), lambda b, i: (b, 0, i, 0)),
                  pl.BlockSpec((1, HEADS, t, HEAD_DIM), lambda b, i: (b, 0, 0, 0)),
                  pl.BlockSpec((1, HEADS, VT_ROWS, t), lambda b, i: (b, 0, 0, 0)),
                  pl.BlockSpec((1, HEADS, nt, HEAD_DIM), lambda b, i: (b, 0, 0, 0)),
                  pl.BlockSpec((1, HEADS, STEP_ROWS, HEAD_DIM), lambda b, i: (b, 0, i, 0)),
                  tile, pl.BlockSpec((1, 1, 3 * d), lambda b, i: (b, 0, 0)),
                  const(d, d), const(1, d)],
        out_specs=tile,
        out_shape=jax.ShapeDtypeStruct((bsz, t, d), F32),
        scratch_shapes=[
            pltpu.VMEM((nt, MOBA_BLOCK, TILE), F32),
            pltpu.VMEM((nt, MOBA_BLOCK, TILE), F32),
            pltpu.VMEM((nt, MOBA_BLOCK, TILE), F32),
            pltpu.VMEM((t, TILE), BF16),
            pltpu.VMEM((t, TILE), BF16),
            pltpu.VMEM((t, TILE), BF16),
            pltpu.VMEM((HEADS, STEP_ROWS, HEAD_DIM), BF16),
        ],
        compiler_params=pltpu.CompilerParams(
            dimension_semantics=("arbitrary", "arbitrary"), vmem_limit_bytes=VMEM_LIMIT),
        name="moba_attention",
    )(q, k, vt, kmean, z, h, mod.reshape(bsz, 1, 3 * d), w_out, post_g.reshape(1, d))


def kernel(x, c, positions, mod_w, mod_b, pre_norm_g, post_norm_g, a_w_in, a_w_out, a_out_norm_g,
           a_lb_logits, kv_norm_g, w_kv, b_w_in, b_w_out):
    assert x.shape[1] % STEP_ROWS == 0 and STEP_ROWS % TILE == 0 and x.shape[2] == HEADS * HEAD_DIM
    assert mod_w.shape[0] == 2 and a_w_in.shape[0] == 1 and b_w_in.shape[0] == 1
    mod = _modulation(c, mod_w, mod_b)
    cosf, sinf = _rope_tables(positions)
    h1 = _layer0(x, mod[0], pre_norm_g[0], a_w_in[0].astype(BF16), a_w_out[0].astype(BF16),
                 a_out_norm_g[0], a_lb_logits, post_norm_g[0])
    d = x.shape[2]
    k, vt, q, z, kmean = _proj1(h1, mod[1], kv_norm_g, pre_norm_g[1], w_kv[:, :d].astype(BF16),
                                w_kv[:, d:].T.astype(BF16), b_w_in[0].astype(BF16), cosf, sinf)
    kmean = jnp.transpose(kmean, (0, 2, 1, 3))
    return _moba(q, k, vt, kmean, z, h1, mod[1], b_w_out[0].astype(BF16), post_norm_g[1])
```

```python
import math

import jax
import jax.numpy as jnp
from jax import lax
from jax.experimental import pallas as pl
from jax.experimental.pallas import tpu as pltpu

F32 = jnp.float32
BF16 = jnp.bfloat16

NORM_EPS = 1e-6
ROPE_THETA = 10000.0
HEADS = 8
HEAD_DIM = 128
MOBA_BLOCK = 256
MOBA_TOPK = 3
TILE = 256
STEP_ROWS = 512
BASE = 8
EXP2_CLAMP = 115.0
NEG = -1e30
LOG2E = 1.4426950408889634
VT_ROWS = HEAD_DIM + 16
LOOKAHEAD = 3
VMEM_LIMIT = 56 * 1024 * 1024


def _nt(a, b):
    return lax.dot_general(a, b, (((1,), (1,)), ((), ())), preferred_element_type=F32)


def _tn(a, b):
    return lax.dot_general(a, b, (((0,), (0,)), ((), ())), preferred_element_type=F32)


def _dot(a, b):
    return jnp.dot(a, b, preferred_element_type=F32)


def _split(x):
    hi = x.astype(BF16)
    lo = (x - hi.astype(F32)).astype(BF16)
    return hi, lo


def _sigmoid(x):
    return 1.0 / (1.0 + jnp.exp2(x * (-LOG2E)))


def _silu(x):
    return x * _sigmoid(x)


def _rms(x):
    return x * lax.rsqrt(jnp.mean(x * x, axis=-1, keepdims=True) + NORM_EPS)


def _mod_kernel(c_ref, w_ref, b_ref, o_ref):
    a = _silu(c_ref[...])
    ah, al = _split(a)
    wh, wl = _split(w_ref[0])
    o_ref[0] = _dot(ah, wh) + (_dot(ah, wl) + _dot(al, wh)) + b_ref[0]


def _modulation(c, mod_w, mod_b):
    depth, d, d3 = mod_w.shape
    bsz = c.shape[0]
    nb = d3 // d
    return pl.pallas_call(
        _mod_kernel,
        grid=(depth, nb),
        in_specs=[
            pl.BlockSpec((bsz, d), lambda l, j: (0, 0)),
            pl.BlockSpec((1, d, d), lambda l, j: (l, 0, j)),
            pl.BlockSpec((1, 1, d), lambda l, j: (l, 0, j)),
        ],
        out_specs=pl.BlockSpec((1, bsz, d), lambda l, j: (l, 0, j)),
        out_shape=jax.ShapeDtypeStruct((depth, bsz, d3), F32),
        compiler_params=pltpu.CompilerParams(vmem_limit_bytes=VMEM_LIMIT),
        name="modulation",
    )(c, mod_w, mod_b.reshape(depth, 1, d3))


def _rope_kernel(pos_ref, cos_ref, sin_ref):
    t2 = pos_ref.shape[1] // 2
    lane = lax.broadcasted_iota(jnp.int32, (1, HEAD_DIM), 1)
    half = HEAD_DIM // 2
    low = lane < half
    i = (lane & (half - 1)).astype(F32)
    inv = jnp.exp(i * (-2.0 / HEAD_DIM * math.log(ROPE_THETA)))
    pos_a = pos_ref[0, 0:t2].astype(F32)
    pos_b = pos_ref[0, t2:2 * t2].astype(F32)
    ang = jnp.where(low, pos_a, pos_b) * inv
    c = jnp.cos(ang)
    s = jnp.sin(ang)
    cr = pltpu.roll(c, half, 1)
    sr = pltpu.roll(s, half, 1)
    cos_ref[0, 0:t2] = jnp.where(low, c, cr)
    cos_ref[0, t2:2 * t2] = jnp.where(low, cr, c)
    sin_ref[0, 0:t2] = jnp.where(low, -s, sr)
    sin_ref[0, t2:2 * t2] = jnp.where(low, -sr, s)


def _rope_tables(positions):
    bsz, t = positions.shape
    out = jax.ShapeDtypeStruct((bsz, t, HEAD_DIM), F32)
    return pl.pallas_call(
        _rope_kernel,
        grid=(bsz,),
        in_specs=[pl.BlockSpec((1, t, 1), lambda b: (b, 0, 0))],
        out_specs=[pl.BlockSpec((1, t, HEAD_DIM), lambda b: (b, 0, 0))] * 2,
        out_shape=[out, out],
        compiler_params=pltpu.CompilerParams(vmem_limit_bytes=VMEM_LIMIT),
        name="rope_tables",
    )(positions.reshape(bsz, t, 1))


def _rope(x, cosf, sinf):
    return x * cosf + pltpu.roll(x, HEAD_DIM // 2, 1) * sinf


def _base_mask():
    r = lax.broadcasted_iota(jnp.int32, (TILE // 2, TILE // 2), 0)
    c = lax.broadcasted_iota(jnp.int32, (TILE // 2, TILE // 2), 1)
    return ((r ^ c) < BASE) & (c <= r)


def _block_masks():
    r = lax.broadcasted_iota(jnp.int32, (TILE // 2, TILE // 2), 0)
    c = lax.broadcasted_iota(jnp.int32, (TILE // 2, TILE // 2), 1)
    x = r ^ c
    masks = {}
    m = BASE
    while m < TILE // 2:
        masks[m] = x < m
        m *= 2
    return masks


def _hgrn2_scores(q, k, v, b, st, base_mask, block_masks):
    half = TILE // 2
    vb = v.astype(BF16)
    nb = TILE // BASE
    b3 = b.reshape(nb, BASE, HEAD_DIM)
    e = (b3 - b3[:, BASE // 2 - 1:BASE // 2, :]).reshape(TILE, HEAD_DIM)
    xq = jnp.exp2(jnp.minimum(e, EXP2_CLAMP))
    xk = jnp.exp2(jnp.minimum(-e, EXP2_CLAMP))
    qx, kx = (q * xq).astype(BF16), (k * xk).astype(BF16)
    base = [jnp.where(base_mask, _nt(qx[r], kx[r]), 0.0).astype(BF16)
            for r in (slice(0, half), slice(half, TILE))]
    probs = []
    m = BASE
    while m < TILE:
        nb = TILE // (2 * m)
        split = lambda a: a.reshape(nb, 2 * m, HEAD_DIM)
        b3 = split(b)
        ref = b3[:, m - 1:m, :]
        qc = (split(q)[:, m:] * jnp.exp2(b3[:, m:] - ref)).reshape(half, HEAD_DIM).astype(BF16)
        kc = (split(k)[:, :m] * jnp.exp2(ref - b3[:, :m])).reshape(half, HEAD_DIM).astype(BF16)
        p = _nt(qc, kc)
        if m in block_masks:
            p = jnp.where(block_masks[m], p, 0.0)
        probs.append(p.astype(BF16))
        m *= 2
    o_state = _nt((q * jnp.exp2(b)).astype(BF16), st.astype(BF16))
    b_end = b[TILE - 1:TILE, :]
    kw = (k * jnp.exp2(b_end - b)).astype(BF16)
    st_new = st * jnp.exp2(b_end) + _tn(vb, kw)
    return base, probs, o_state, st_new


def _hgrn2_values(base, probs, o_state, v):
    half = TILE // 2
    vb = v.astype(BF16)
    o = o_state + jnp.concatenate([_dot(base[0], vb[0:half]), _dot(base[1], vb[half:TILE])], axis=0)
    rows = [o[r:r + 8] for r in range(0, TILE, 8)]
    m = BASE
    for p in probs:
        nb = TILE // (2 * m)
        src = vb if m % 16 == 0 else v
        vc = src.reshape(nb, 2 * m, HEAD_DIM)[:, :m].reshape(half, HEAD_DIM).astype(BF16)
        oc = _dot(p, vc)
        for j in range(0, half, 8):
            t = (j // m) * 2 * m + m + j % m
            rows[t // 8] = rows[t // 8] + oc[j:j + 8]
        m *= 2
    return jnp.concatenate(rows, axis=0)


def _layer0_kernel(x_ref, mod_ref, png_ref, win_ref, wout_ref, ong_ref, lbl_ref, pong_ref,
                   o_ref, st_ref, og_ref):
    @pl.when(pl.program_id(1) == 0)
    def _():
        st_ref[...] = jnp.zeros_like(st_ref)

    d = x_ref.shape[-1]
    x = x_ref[0]
    mod = mod_ref[0]
    shift, scale, gate = mod[:, :d], mod[:, d:2 * d], mod[:, 2 * d:]
    u = (_rms(x) * png_ref[...]) * (1.0 + scale) + shift
    ub = u.astype(BF16)

    lbl = lbl_ref[...]
    ex = jnp.exp(lbl - jnp.max(lbl, axis=0, keepdims=True))
    lb = ex[0:1, :] / jnp.sum(ex, axis=0, keepdims=True)

    fg = lb + (1.0 - lb) * _sigmoid(_dot(ub, win_ref[:, d:2 * d]))
    qa = _silu(_dot(ub, win_ref[:, 0:d]))
    va = _dot(ub, win_ref[:, 2 * d:3 * d])
    ga = _silu(_dot(ub, win_ref[:, 3 * d:4 * d])) * jnp.concatenate([ong_ref[...]] * HEADS, axis=1)
    ka = 1.0 - fg
    lf_hi, lf_lo = _split(jnp.log2(fg))
    r = lax.broadcasted_iota(jnp.int32, (TILE, TILE), 0)
    c = lax.broadcasted_iota(jnp.int32, (TILE, TILE), 1)
    tri = (c <= r).astype(BF16)
    chunks = [slice(j * TILE, (j + 1) * TILE) for j in range(x.shape[0] // TILE)]
    bcum = [_dot(tri, lf_hi[rows]) + _dot(tri, lf_lo[rows]) for rows in chunks]

    base_mask, block_masks = _base_mask(), _block_masks()
    head = lambda h: slice(h * HEAD_DIM, (h + 1) * HEAD_DIM)
    units = [(j, h) for j in range(len(chunks)) for h in range(HEADS)]

    def scores(unit):
        j, h = unit
        rows, hs = chunks[j], head(h)
        base, probs, o_state, st_new = _hgrn2_scores(qa[rows, hs], ka[rows, hs], va[rows, hs], bcum[j][:, hs],
                                                     st_ref[h], base_mask, block_masks)
        st_ref[h] = st_new
        return base, probs, o_state

    staged = scores(units[0])
    for n, (j, h) in enumerate(units):
        staged_next = scores(units[n + 1]) if n + 1 < len(units) else None
        rows, hs = chunks[j], head(h)
        o = _hgrn2_values(*staged, va[rows, hs])
        og_ref[rows, hs] = (_rms(o) * ga[rows, hs]).astype(BF16)
        staged = staged_next

    y = _dot(og_ref[...], wout_ref[...])
    o_ref[0] = x + gate * (_rms(y) * pong_ref[...])


def _layer0(x, mod, pre_g, w_in, w_out, out_g, lb_logits, post_g):
    bsz, t, d = x.shape
    nt = t // STEP_ROWS
    const = lambda *shape: pl.BlockSpec(shape, lambda b, i: (0,) * len(shape))
    return pl.pallas_call(
        _layer0_kernel,
        grid=(bsz, nt),
        in_specs=[
            pl.BlockSpec((1, STEP_ROWS, d), lambda b, i: (b, i, 0)),
            pl.BlockSpec((1, 1, 3 * d), lambda b, i: (b, 0, 0)),
            const(1, d),
            const(d, 4 * d),
            const(d, d),
            const(1, HEAD_DIM),
            const(lb_logits.shape[0], d),
            const(1, d),
        ],
        out_specs=pl.BlockSpec((1, STEP_ROWS, d), lambda b, i: (b, i, 0)),
        out_shape=jax.ShapeDtypeStruct((bsz, t, d), F32),
        scratch_shapes=[
            pltpu.VMEM((HEADS, HEAD_DIM, HEAD_DIM), F32),
            pltpu.VMEM((STEP_ROWS, d), BF16),
        ],
        compiler_params=pltpu.CompilerParams(
            dimension_semantics=("arbitrary", "arbitrary"), vmem_limit_bytes=VMEM_LIMIT),
        name="hgrn2_layer",
    )(x, mod.reshape(bsz, 1, 3 * d), pre_g.reshape(1, d), w_in, w_out,
      out_g.reshape(1, HEAD_DIM), lb_logits, post_g.reshape(1, d))


def _proj1_kernel(h_ref, mod_ref, kvg_ref, png_ref, wk_ref, wvt_ref, win_ref, cos_ref, sin_ref,
                  k_ref, vt_ref, q_ref, z_ref, km_ref):
    d = h_ref.shape[-1]
    xn = _rms(h_ref[0])
    mod = mod_ref[0]
    shift, scale = mod[:, :d], mod[:, d:2 * d]
    a = (xn * kvg_ref[...]).astype(BF16)
    u = ((xn * png_ref[...]) * (1.0 + scale) + shift).astype(BF16)
    cosf, sinf = cos_ref[0], sin_ref[0]
    kk = _dot(a, wk_ref[...])
    qq = _dot(u, win_ref[:, 0:d])
    qscale = HEAD_DIM ** -0.5 * LOG2E
    vt = _nt(wvt_ref[...], a).astype(BF16)
    rows = h_ref.shape[1]
    ones = jnp.ones((VT_ROWS - HEAD_DIM, rows), BF16)
    for h in range(HEADS):
        hs = slice(h * HEAD_DIM, (h + 1) * HEAD_DIM)
        kr = _rope(kk[:, hs], cosf, sinf)
        k_ref[0, h] = kr.astype(BF16)
        for j in range(rows // MOBA_BLOCK):
            km_ref[0, j, h:h + 1, :] = jnp.mean(kr[j * MOBA_BLOCK:(j + 1) * MOBA_BLOCK], axis=0, keepdims=True)
        q_ref[0, h] = (_rope(qq[:, hs], cosf, sinf) * qscale).astype(BF16)
        vt_ref[0, h, 0:HEAD_DIM, :] = vt[hs, :]
        vt_ref[0, h, HEAD_DIM:VT_ROWS, :] = ones
    zz = _dot(u, win_ref[:, d:2 * d]).astype(BF16)
    for h in range(HEADS):
        z_ref[0, h] = zz[:, h * HEAD_DIM:(h + 1) * HEAD_DIM]


def _proj1(h, mod, kv_g, pre_g, w_k, w_vt, w_in, cosf, sinf):
    bsz, t, d = h.shape
    nt = t // STEP_ROWS
    const = lambda *shape: pl.BlockSpec(shape, lambda b, i: (0,) * len(shape))
    tile = pl.BlockSpec((1, STEP_ROWS, d), lambda b, i: (b, i, 0))
    rope = pl.BlockSpec((1, STEP_ROWS, HEAD_DIM), lambda b, i: (b, i, 0))
    heads = pl.BlockSpec((1, HEADS, STEP_ROWS, HEAD_DIM), lambda b, i: (b, 0, i, 0))
    heads_shape = jax.ShapeDtypeStruct((bsz, HEADS, t, HEAD_DIM), BF16)
    return pl.pallas_call(
        _proj1_kernel,
        grid=(bsz, nt),
        in_specs=[tile, pl.BlockSpec((1, 1, 3 * d), lambda b, i: (b, 0, 0)),
                  const(1, d), const(1, d), const(d, d), const(d, d), const(d, 2 * d), rope, rope],
        out_specs=[heads, pl.BlockSpec((1, HEADS, VT_ROWS, STEP_ROWS), lambda b, i: (b, 0, 0, i)), heads, heads,
                   pl.BlockSpec((1, STEP_ROWS // MOBA_BLOCK, HEADS, HEAD_DIM), lambda b, i: (b, i, 0, 0))],
        out_shape=[heads_shape, jax.ShapeDtypeStruct((bsz, HEADS, VT_ROWS, t), BF16), heads_shape, heads_shape,
                   jax.ShapeDtypeStruct((bsz, t // MOBA_BLOCK, HEADS, HEAD_DIM), F32)],
        compiler_params=pltpu.CompilerParams(
            dimension_semantics=("arbitrary", "arbitrary"), vmem_limit_bytes=VMEM_LIMIT),
        name="moba_proj",
    )(h, mod.reshape(bsz, 1, 3 * d), kv_g.reshape(1, d), pre_g.reshape(1, d), w_k, w_vt, w_in, cosf, sinf)


def _moba_scores(h, rows, nb, q_ref, k_ref, km_ref, s_ref, causal, blk):
    npast = nb - 1
    groups = (MOBA_BLOCK // 8, 8, TILE)
    q = q_ref[0, h, rows, :]
    keys = k_ref[0, h, 0:nb * MOBA_BLOCK, :]
    bias = None
    off = 0
    if npast > MOBA_TOPK:
        km = km_ref[0, h]
        km_hi = km.astype(BF16).astype(F32)
        keys = jnp.concatenate([jnp.concatenate([km_hi, km - km_hi], axis=0).astype(BF16), keys], axis=0)
        off = 2 * km.shape[0]
    s_all = _nt(keys, q)
    if off:
        g = s_all[0:off // 2] + s_all[off // 2:off]
        g = jnp.where(blk < npast, g, -jnp.inf)
        rank = jnp.zeros(g.shape, jnp.int32)
        for m in range(npast):
            gm = g[m:m + 1, :]
            rank = rank + ((gm > g) | ((gm == g) & (m < blk))).astype(jnp.int32)
        bias = jnp.where(rank < MOBA_TOPK, 0.0, NEG)

    m8 = None
    for n in range(nb):
        s = s_all[off + n * MOBA_BLOCK:off + (n + 1) * MOBA_BLOCK]
        if n == npast:
            s = jnp.where(causal, s, NEG)
        elif bias is not None:
            s = s + bias[n:n + 1, :]
        s_ref[n] = s
        mb = jnp.max(s.reshape(groups), axis=0)
        m8 = mb if m8 is None else jnp.maximum(m8, mb)
    return jnp.max(m8, axis=0, keepdims=True)


def _moba_values(h, rows, nb, m, vt_ref, z_ref, s_ref, p_ref, og_ref):
    for n in range(nb):
        p_ref[n * MOBA_BLOCK:(n + 1) * MOBA_BLOCK, :] = jnp.exp2(s_ref[n] - m).astype(BF16)
    acc = _dot(vt_ref[0, h, :, 0:nb * MOBA_BLOCK], p_ref[0:nb * MOBA_BLOCK, :])
    o = (acc[0:HEAD_DIM] * (1.0 / acc[HEAD_DIM:HEAD_DIM + 1])).T
    og_ref[h, rows, :] = (o * _silu(z_ref[0, h, rows, :].astype(F32))).astype(BF16)


def _moba_kernel(q_ref, k_ref, vt_ref, km_ref, z_ref, h_ref, mod_ref, wout_ref, pong_ref,
                 o_ref, sa_ref, sb_ref, sc_ref, sd_ref, pa_ref, pb_ref, pc_ref, pd_ref, og_ref):
    d = h_ref.shape[-1]
    i = pl.program_id(1)
    nblk = km_ref.shape[2]
    tiles = h_ref.shape[1] // TILE
    key = lax.broadcasted_iota(jnp.int32, (MOBA_BLOCK, TILE), 0)
    qry = lax.broadcasted_iota(jnp.int32, (MOBA_BLOCK, TILE), 1)
    causal = key <= qry
    blk = lax.broadcasted_iota(jnp.int32, (nblk, TILE), 0)

    for step in range(nblk // tiles):
        @pl.when(i == step)
        def _(step=step):
            units = [(t, h) for t in range(tiles) for h in range(HEADS)]

            def scores(n):
                t, h = units[n]
                return _moba_scores(h, slice(t * TILE, (t + 1) * TILE), step * tiles + t + 1, q_ref, k_ref, km_ref,
                                    bufs[n % len(bufs)][0], causal, blk)

            def values(n, m):
                t, h = units[n]
                _moba_values(h, slice(t * TILE, (t + 1) * TILE), step * tiles + t + 1, m, vt_ref, z_ref,
                             *bufs[n % len(bufs)], og_ref)

            bufs = ((sa_ref, pa_ref), (sb_ref, pb_ref), (sc_ref, pc_ref), (sd_ref, pd_ref))
            maxes = {n: scores(n) for n in range(LOOKAHEAD)}
            for n in range(len(units)):
                if n + LOOKAHEAD < len(units):
                    maxes[n + LOOKAHEAD] = scores(n + LOOKAHEAD)
                values(n, maxes.pop(n))

    y = _dot(jnp.concatenate([og_ref[h] for h in range(HEADS)], axis=1), wout_ref[...])
    gate = mod_ref[0][:, 2 * d:]
    o_ref[0] = h_ref[0] + gate * (_rms(y) * pong_ref[...])


def _moba(q, k, vt, kmean, z, h, mod, w_out, post_g):
    bsz, t, d = h.shape
    nt = t // TILE
    const = lambda *shape: pl.BlockSpec(shape, lambda b, i: (0,) * len(shape))
    tile = pl.BlockSpec((1, STEP_ROWS, d), lambda b, i: (b, i, 0))
    return pl.pallas_call(
        _moba_kernel,
        grid=(bsz, t // STEP_ROWS),
        in_specs=[pl.BlockSpec((1, HEADS, STEP_ROWS, HEAD_DIM), lambda b, i: (b, 0, i, 0)),
                  pl.BlockSpec((1, HEADS, t, HEAD_DIM), lambda b, i: (b, 0, 0, 0)),
                  pl.BlockSpec((1, HEADS, VT_ROWS, t), lambda b, i: (b, 0, 0, 0)),
                  pl.BlockSpec((1, HEADS, nt, HEAD_DIM), lambda b, i: (b, 0, 0, 0)),
                  pl.BlockSpec((1, HEADS, STEP_ROWS, HEAD_DIM), lambda b, i: (b, 0, i, 0)),
                  tile, pl.BlockSpec((1, 1, 3 * d), lambda b, i: (b, 0, 0)),
                  const(d, d), const(1, d)],
        out_specs=tile,
        out_shape=jax.ShapeDtypeStruct((bsz, t, d), F32),
        scratch_shapes=[
            *[pltpu.VMEM((nt, MOBA_BLOCK, TILE), F32)] * (LOOKAHEAD + 1),
            *[pltpu.VMEM((t, TILE), BF16)] * (LOOKAHEAD + 1),
            pltpu.VMEM((HEADS, STEP_ROWS, HEAD_DIM), BF16),
        ],
        compiler_params=pltpu.CompilerParams(
            dimension_semantics=("arbitrary", "arbitrary"), vmem_limit_bytes=VMEM_LIMIT),
        name="moba_attention",
    )(q, k, vt, kmean, z, h, mod.reshape(bsz, 1, 3 * d), w_out, post_g.reshape(1, d))


def kernel(x, c, positions, mod_w, mod_b, pre_norm_g, post_norm_g, a_w_in, a_w_out, a_out_norm_g,
           a_lb_logits, kv_norm_g, w_kv, b_w_in, b_w_out):
    assert x.shape[1] % STEP_ROWS == 0 and STEP_ROWS % TILE == 0 and x.shape[2] == HEADS * HEAD_DIM
    assert mod_w.shape[0] == 2 and a_w_in.shape[0] == 1 and b_w_in.shape[0] == 1
    mod = _modulation(c, mod_w, mod_b)
    cosf, sinf = _rope_tables(positions)
    h1 = _layer0(x, mod[0], pre_norm_g[0], a_w_in[0].astype(BF16), a_w_out[0].astype(BF16),
                 a_out_norm_g[0], a_lb_logits, post_norm_g[0])
    d = x.shape[2]
    k, vt, q, z, kmean = _proj1(h1, mod[1], kv_norm_g, pre_norm_g[1], w_kv[:, :d].astype(BF16),
                                w_kv[:, d:].astype(BF16).T, b_w_in[0].astype(BF16), cosf, sinf)
    kmean = jnp.transpose(kmean, (0, 2, 1, 3))
    return _moba(q, k, vt, kmean, z, h1, mod[1], b_w_out[0].astype(BF16), post_norm_g[1])
```

```python
import math

import jax
import jax.numpy as jnp
from jax import lax
from jax.experimental import pallas as pl
from jax.experimental.pallas import tpu as pltpu

F32 = jnp.float32
BF16 = jnp.bfloat16

NORM_EPS = 1e-6
ROPE_THETA = 10000.0
HEADS = 8
HEAD_DIM = 128
MOBA_BLOCK = 256
MOBA_TOPK = 3
TILE = 256
STEP_ROWS = 512
BASE = 8
EXP2_CLAMP = 115.0
NEG = -1e30
LOG2E = 1.4426950408889634
VT_ROWS = HEAD_DIM + 16
LOOKAHEAD = 3
VMEM_LIMIT = 56 * 1024 * 1024


def _nt(a, b):
    return lax.dot_general(a, b, (((1,), (1,)), ((), ())), preferred_element_type=F32)


def _tn(a, b):
    return lax.dot_general(a, b, (((0,), (0,)), ((), ())), preferred_element_type=F32)


def _dot(a, b):
    return jnp.dot(a, b, preferred_element_type=F32)


def _split(x):
    hi = x.astype(BF16)
    lo = (x - hi.astype(F32)).astype(BF16)
    return hi, lo


def _sigmoid(x):
    return 1.0 / (1.0 + jnp.exp2(x * (-LOG2E)))


def _silu(x):
    return x * _sigmoid(x)


def _rms(x):
    return x * lax.rsqrt(jnp.mean(x * x, axis=-1, keepdims=True) + NORM_EPS)


def _mod_kernel(c_ref, w_ref, b_ref, o_ref):
    a = _silu(c_ref[...])
    ah, al = _split(a)
    wh, wl = _split(w_ref[0])
    o_ref[0] = _dot(ah, wh) + (_dot(ah, wl) + _dot(al, wh)) + b_ref[0]


def _modulation(c, mod_w, mod_b):
    depth, d, d3 = mod_w.shape
    bsz = c.shape[0]
    nb = d3 // d
    return pl.pallas_call(
        _mod_kernel,
        grid=(depth, nb),
        in_specs=[
            pl.BlockSpec((bsz, d), lambda l, j: (0, 0)),
            pl.BlockSpec((1, d, d), lambda l, j: (l, 0, j)),
            pl.BlockSpec((1, 1, d), lambda l, j: (l, 0, j)),
        ],
        out_specs=pl.BlockSpec((1, bsz, d), lambda l, j: (l, 0, j)),
        out_shape=jax.ShapeDtypeStruct((depth, bsz, d3), F32),
        compiler_params=pltpu.CompilerParams(vmem_limit_bytes=VMEM_LIMIT),
        name="modulation",
    )(c, mod_w, mod_b.reshape(depth, 1, d3))


def _rope_kernel(pos_ref, cos_ref, sin_ref):
    t2 = pos_ref.shape[1] // 2
    lane = lax.broadcasted_iota(jnp.int32, (1, HEAD_DIM), 1)
    half = HEAD_DIM // 2
    low = lane < half
    i = (lane & (half - 1)).astype(F32)
    inv = jnp.exp(i * (-2.0 / HEAD_DIM * math.log(ROPE_THETA)))
    pos_a = pos_ref[0, 0:t2].astype(F32)
    pos_b = pos_ref[0, t2:2 * t2].astype(F32)
    ang = jnp.where(low, pos_a, pos_b) * inv
    c = jnp.cos(ang)
    s = jnp.sin(ang)
    cr = pltpu.roll(c, half, 1)
    sr = pltpu.roll(s, half, 1)
    cos_ref[0, 0:t2] = jnp.where(low, c, cr)
    cos_ref[0, t2:2 * t2] = jnp.where(low, cr, c)
    sin_ref[0, 0:t2] = jnp.where(low, -s, sr)
    sin_ref[0, t2:2 * t2] = jnp.where(low, -sr, s)


def _rope_tables(positions):
    bsz, t = positions.shape
    out = jax.ShapeDtypeStruct((bsz, t, HEAD_DIM), F32)
    return pl.pallas_call(
        _rope_kernel,
        grid=(bsz,),
        in_specs=[pl.BlockSpec((1, t, 1), lambda b: (b, 0, 0))],
        out_specs=[pl.BlockSpec((1, t, HEAD_DIM), lambda b: (b, 0, 0))] * 2,
        out_shape=[out, out],
        compiler_params=pltpu.CompilerParams(vmem_limit_bytes=VMEM_LIMIT),
        name="rope_tables",
    )(positions.reshape(bsz, t, 1))


def _rope(x, cosf, sinf):
    return x * cosf + pltpu.roll(x, HEAD_DIM // 2, 1) * sinf


def _base_mask():
    r = lax.broadcasted_iota(jnp.int32, (TILE // 2, TILE // 2), 0)
    c = lax.broadcasted_iota(jnp.int32, (TILE // 2, TILE // 2), 1)
    return ((r ^ c) < BASE) & (c <= r)


def _block_masks():
    r = lax.broadcasted_iota(jnp.int32, (TILE // 2, TILE // 2), 0)
    c = lax.broadcasted_iota(jnp.int32, (TILE // 2, TILE // 2), 1)
    x = r ^ c
    masks = {}
    m = BASE
    while m < TILE // 2:
        masks[m] = x < m
        m *= 2
    return masks


def _hgrn2_scores(q, k, v, b, st, base_mask, block_masks):
    half = TILE // 2
    vb = v.astype(BF16)
    nb = TILE // BASE
    b3 = b.reshape(nb, BASE, HEAD_DIM)
    e = (b3 - b3[:, BASE // 2 - 1:BASE // 2, :]).reshape(TILE, HEAD_DIM)
    xq = jnp.exp2(jnp.minimum(e, EXP2_CLAMP))
    xk = jnp.exp2(jnp.minimum(-e, EXP2_CLAMP))
    qx, kx = (q * xq).astype(BF16), (k * xk).astype(BF16)
    base = [jnp.where(base_mask, _nt(qx[r], kx[r]), 0.0).astype(BF16)
            for r in (slice(0, half), slice(half, TILE))]
    probs = []
    m = BASE
    while m < TILE:
        nb = TILE // (2 * m)
        split = lambda a: a.reshape(nb, 2 * m, HEAD_DIM)
        b3 = split(b)
        ref = b3[:, m - 1:m, :]
        qc = (split(q)[:, m:] * jnp.exp2(b3[:, m:] - ref)).reshape(half, HEAD_DIM).astype(BF16)
        kc = (split(k)[:, :m] * jnp.exp2(ref - b3[:, :m])).reshape(half, HEAD_DIM).astype(BF16)
        p = _nt(qc, kc)
        if m in block_masks:
            p = jnp.where(block_masks[m], p, 0.0)
        probs.append(p.astype(BF16))
        m *= 2
    o_state = _nt((q * jnp.exp2(b)).astype(BF16), st.astype(BF16))
    b_end = b[TILE - 1:TILE, :]
    kw = (k * jnp.exp2(b_end - b)).astype(BF16)
    st_new = st * jnp.exp2(b_end) + _tn(vb, kw)
    return base, probs, o_state, st_new


def _hgrn2_values(base, probs, o_state, v):
    half = TILE // 2
    vb = v.astype(BF16)
    o = o_state + jnp.concatenate([_dot(base[0], vb[0:half]), _dot(base[1], vb[half:TILE])], axis=0)
    rows = [o[r:r + 8] for r in range(0, TILE, 8)]
    m = BASE
    for p in probs:
        nb = TILE // (2 * m)
        src = vb if m % 16 == 0 else v
        vc = src.reshape(nb, 2 * m, HEAD_DIM)[:, :m].reshape(half, HEAD_DIM).astype(BF16)
        oc = _dot(p, vc)
        for j in range(0, half, 8):
            t = (j // m) * 2 * m + m + j % m
            rows[t // 8] = rows[t // 8] + oc[j:j + 8]
        m *= 2
    return jnp.concatenate(rows, axis=0)


def _layer0_kernel(x_ref, mod_ref, png_ref, win_ref, wout_ref, ong_ref, lbl_ref, pong_ref,
                   o_ref, st_ref, og_ref):
    @pl.when(pl.program_id(1) == 0)
    def _():
        st_ref[...] = jnp.zeros_like(st_ref)

    d = x_ref.shape[-1]
    x = x_ref[0]
    mod = mod_ref[0]
    shift, scale, gate = mod[:, :d], mod[:, d:2 * d], mod[:, 2 * d:]
    u = (_rms(x) * png_ref[...]) * (1.0 + scale) + shift
    ub = u.astype(BF16)

    lbl = lbl_ref[...]
    ex = jnp.exp(lbl - jnp.max(lbl, axis=0, keepdims=True))
    lb = ex[0:1, :] / jnp.sum(ex, axis=0, keepdims=True)

    fg = lb + (1.0 - lb) * _sigmoid(_dot(ub, win_ref[:, d:2 * d]))
    qa = _silu(_dot(ub, win_ref[:, 0:d]))
    va = _dot(ub, win_ref[:, 2 * d:3 * d])
    ga = _silu(_dot(ub, win_ref[:, 3 * d:4 * d])) * jnp.concatenate([ong_ref[...]] * HEADS, axis=1)
    ka = 1.0 - fg
    lf_hi, lf_lo = _split(jnp.log2(fg))
    r = lax.broadcasted_iota(jnp.int32, (TILE, TILE), 0)
    c = lax.broadcasted_iota(jnp.int32, (TILE, TILE), 1)
    tri = (c <= r).astype(BF16)
    chunks = [slice(j * TILE, (j + 1) * TILE) for j in range(x.shape[0] // TILE)]
    bcum = [_dot(tri, lf_hi[rows]) + _dot(tri, lf_lo[rows]) for rows in chunks]

    base_mask, block_masks = _base_mask(), _block_masks()
    head = lambda h: slice(h * HEAD_DIM, (h + 1) * HEAD_DIM)
    units = [(j, h) for j in range(len(chunks)) for h in range(HEADS)]

    def scores(unit):
        j, h = unit
        rows, hs = chunks[j], head(h)
        base, probs, o_state, st_new = _hgrn2_scores(qa[rows, hs], ka[rows, hs], va[rows, hs], bcum[j][:, hs],
                                                     st_ref[h], base_mask, block_masks)
        st_ref[h] = st_new
        return base, probs, o_state

    staged = scores(units[0])
    for n, (j, h) in enumerate(units):
        staged_next = scores(units[n + 1]) if n + 1 < len(units) else None
        rows, hs = chunks[j], head(h)
        o = _hgrn2_values(*staged, va[rows, hs])
        og_ref[rows, hs] = (_rms(o) * ga[rows, hs]).astype(BF16)
        staged = staged_next

    y = _dot(og_ref[...], wout_ref[...])
    o_ref[0] = x + gate * (_rms(y) * pong_ref[...])


def _layer0(x, mod, pre_g, w_in, w_out, out_g, lb_logits, post_g):
    bsz, t, d = x.shape
    nt = t // STEP_ROWS
    const = lambda *shape: pl.BlockSpec(shape, lambda b, i: (0,) * len(shape))
    return pl.pallas_call(
        _layer0_kernel,
        grid=(bsz, nt),
        in_specs=[
            pl.BlockSpec((1, STEP_ROWS, d), lambda b, i: (b, i, 0)),
            pl.BlockSpec((1, 1, 3 * d), lambda b, i: (b, 0, 0)),
            const(1, d),
            const(d, 4 * d),
            const(d, d),
            const(1, HEAD_DIM),
            const(lb_logits.shape[0], d),
            const(1, d),
        ],
        out_specs=pl.BlockSpec((1, STEP_ROWS, d), lambda b, i: (b, i, 0)),
        out_shape=jax.ShapeDtypeStruct((bsz, t, d), F32),
        scratch_shapes=[
            pltpu.VMEM((HEADS, HEAD_DIM, HEAD_DIM), F32),
            pltpu.VMEM((STEP_ROWS, d), BF16),
        ],
        compiler_params=pltpu.CompilerParams(
            dimension_semantics=("arbitrary", "arbitrary"), vmem_limit_bytes=VMEM_LIMIT),
        name="hgrn2_layer",
    )(x, mod.reshape(bsz, 1, 3 * d), pre_g.reshape(1, d), w_in, w_out,
      out_g.reshape(1, HEAD_DIM), lb_logits, post_g.reshape(1, d))


def _proj1_kernel(h_ref, mod_ref, kvg_ref, png_ref, wk_ref, wvt_ref, win_ref, cos_ref, sin_ref,
                  k_ref, vt_ref, q_ref, z_ref, km_ref):
    d = h_ref.shape[-1]
    xn = _rms(h_ref[0])
    mod = mod_ref[0]
    shift, scale = mod[:, :d], mod[:, d:2 * d]
    a = (xn * kvg_ref[...]).astype(BF16)
    u = ((xn * png_ref[...]) * (1.0 + scale) + shift).astype(BF16)
    cosf, sinf = cos_ref[0], sin_ref[0]
    kk = _dot(a, wk_ref[...])
    qq = _dot(u, win_ref[:, 0:d])
    qscale = HEAD_DIM ** -0.5 * LOG2E
    vt = _nt(wvt_ref[...], a).astype(BF16)
    rows = h_ref.shape[1]
    ones = jnp.ones((VT_ROWS - HEAD_DIM, rows), BF16)
    for h in range(HEADS):
        hs = slice(h * HEAD_DIM, (h + 1) * HEAD_DIM)
        kr = _rope(kk[:, hs], cosf, sinf)
        k_ref[0, h] = kr.astype(BF16)
        for j in range(rows // MOBA_BLOCK):
            km_ref[0, j, h:h + 1, :] = jnp.mean(kr[j * MOBA_BLOCK:(j + 1) * MOBA_BLOCK], axis=0, keepdims=True)
        q_ref[0, h] = (_rope(qq[:, hs], cosf, sinf) * qscale).astype(BF16)
        vt_ref[0, h, 0:HEAD_DIM, :] = vt[hs, :]
        vt_ref[0, h, HEAD_DIM:VT_ROWS, :] = ones
    zz = _dot(u, win_ref[:, d:2 * d]).astype(BF16)
    for h in range(HEADS):
        z_ref[0, h] = zz[:, h * HEAD_DIM:(h + 1) * HEAD_DIM]


def _proj1(h, mod, kv_g, pre_g, w_k, w_vt, w_in, cosf, sinf):
    bsz, t, d = h.shape
    nt = t // STEP_ROWS
    const = lambda *shape: pl.BlockSpec(shape, lambda b, i: (0,) * len(shape))
    tile = pl.BlockSpec((1, STEP_ROWS, d), lambda b, i: (b, i, 0))
    rope = pl.BlockSpec((1, STEP_ROWS, HEAD_DIM), lambda b, i: (b, i, 0))
    heads = pl.BlockSpec((1, HEADS, STEP_ROWS, HEAD_DIM), lambda b, i: (b, 0, i, 0))
    heads_shape = jax.ShapeDtypeStruct((bsz, HEADS, t, HEAD_DIM), BF16)
    return pl.pallas_call(
        _proj1_kernel,
        grid=(bsz, nt),
        in_specs=[tile, pl.BlockSpec((1, 1, 3 * d), lambda b, i: (b, 0, 0)),
                  const(1, d), const(1, d), const(d, d), const(d, d), const(d, 2 * d), rope, rope],
        out_specs=[heads, pl.BlockSpec((1, HEADS, VT_ROWS, STEP_ROWS), lambda b, i: (b, 0, 0, i)), heads, heads,
                   pl.BlockSpec((1, STEP_ROWS // MOBA_BLOCK, HEADS, HEAD_DIM), lambda b, i: (b, i, 0, 0))],
        out_shape=[heads_shape, jax.ShapeDtypeStruct((bsz, HEADS, VT_ROWS, t), BF16), heads_shape, heads_shape,
                   jax.ShapeDtypeStruct((bsz, t // MOBA_BLOCK, HEADS, HEAD_DIM), F32)],
        compiler_params=pltpu.CompilerParams(
            dimension_semantics=("arbitrary", "arbitrary"), vmem_limit_bytes=VMEM_LIMIT),
        name="moba_proj",
    )(h, mod.reshape(bsz, 1, 3 * d), kv_g.reshape(1, d), pre_g.reshape(1, d), w_k, w_vt, w_in, cosf, sinf)


def _moba_scores(h, rows, nb, q_ref, k_ref, km_ref, s_ref, causal, blk):
    npast = nb - 1
    groups = (MOBA_BLOCK // 8, 8, TILE)
    q = q_ref[0, h, rows, :]
    keys = k_ref[0, h, 0:nb * MOBA_BLOCK, :]
    bias = None
    off = 0
    if npast > MOBA_TOPK:
        km = km_ref[0, h]
        km_hi = km.astype(BF16).astype(F32)
        keys = jnp.concatenate([jnp.concatenate([km_hi, km - km_hi], axis=0).astype(BF16), keys], axis=0)
        off = 2 * km.shape[0]
    s_all = _nt(keys, q)
    if off:
        g = s_all[0:off // 2] + s_all[off // 2:off]
        g = jnp.where(blk < npast, g, -jnp.inf)
        rank = jnp.zeros(g.shape, jnp.int32)
        for m in range(npast):
            gm = g[m:m + 1, :]
            rank = rank + ((gm > g) | ((gm == g) & (m < blk))).astype(jnp.int32)
        bias = jnp.where(rank < MOBA_TOPK, 0.0, NEG)

    m8 = None
    for n in range(nb):
        s = s_all[off + n * MOBA_BLOCK:off + (n + 1) * MOBA_BLOCK]
        if n == npast:
            s = jnp.where(causal, s, NEG)
        elif bias is not None:
            s = s + bias[n:n + 1, :]
        s_ref[n] = s
        mb = jnp.max(s.reshape(groups), axis=0)
        m8 = mb if m8 is None else jnp.maximum(m8, mb)
    return jnp.max(m8, axis=0, keepdims=True)


def _moba_values(h, rows, nb, m, vt_ref, z_ref, s_ref, og_ref):
    p = jnp.concatenate([jnp.exp2(s_ref[n] - m).astype(BF16) for n in range(nb)], axis=0)
    acc = _dot(vt_ref[0, h, :, 0:nb * MOBA_BLOCK], p)
    o = (acc[0:HEAD_DIM] * (1.0 / acc[HEAD_DIM:HEAD_DIM + 1])).T
    og_ref[h, rows, :] = (o * _silu(z_ref[0, h, rows, :].astype(F32))).astype(BF16)


def _moba_kernel(q_ref, k_ref, vt_ref, km_ref, z_ref, h_ref, mod_ref, wout_ref, pong_ref,
                 o_ref, *scratch):
    bufs, og_ref = scratch[:-1], scratch[-1]
    d = h_ref.shape[-1]
    i = pl.program_id(1)
    nblk = km_ref.shape[2]
    tiles = h_ref.shape[1] // TILE
    key = lax.broadcasted_iota(jnp.int32, (MOBA_BLOCK, TILE), 0)
    qry = lax.broadcasted_iota(jnp.int32, (MOBA_BLOCK, TILE), 1)
    causal = key <= qry
    blk = lax.broadcasted_iota(jnp.int32, (nblk, TILE), 0)

    for step in range(nblk // tiles):
        @pl.when(i == step)
        def _(step=step):
            units = [(t, h) for t in range(tiles) for h in range(HEADS)]

            def scores(n):
                t, h = units[n]
                return _moba_scores(h, slice(t * TILE, (t + 1) * TILE), step * tiles + t + 1, q_ref, k_ref, km_ref,
                                    bufs[n % len(bufs)], causal, blk)

            def values(n, m):
                t, h = units[n]
                _moba_values(h, slice(t * TILE, (t + 1) * TILE), step * tiles + t + 1, m, vt_ref, z_ref,
                             bufs[n % len(bufs)], og_ref)

            maxes = {n: scores(n) for n in range(LOOKAHEAD)}
            for n in range(len(units)):
                if n + LOOKAHEAD < len(units):
                    maxes[n + LOOKAHEAD] = scores(n + LOOKAHEAD)
                values(n, maxes.pop(n))

    y = _dot(jnp.concatenate([og_ref[h] for h in range(HEADS)], axis=1), wout_ref[...])
    gate = mod_ref[0][:, 2 * d:]
    o_ref[0] = h_ref[0] + gate * (_rms(y) * pong_ref[...])


def _moba(q, k, vt, kmean, z, h, mod, w_out, post_g):
    bsz, t, d = h.shape
    nt = t // TILE
    const = lambda *shape: pl.BlockSpec(shape, lambda b, i: (0,) * len(shape))
    tile = pl.BlockSpec((1, STEP_ROWS, d), lambda b, i: (b, i, 0))
    return pl.pallas_call(
        _moba_kernel,
        grid=(bsz, t // STEP_ROWS),
        in_specs=[pl.BlockSpec((1, HEADS, STEP_ROWS, HEAD_DIM), lambda b, i: (b, 0, i, 0)),
                  pl.BlockSpec((1, HEADS, t, HEAD_DIM), lambda b, i: (b, 0, 0, 0)),
                  pl.BlockSpec((1, HEADS, VT_ROWS, t), lambda b, i: (b, 0, 0, 0)),
                  pl.BlockSpec((1, HEADS, nt, HEAD_DIM), lambda b, i: (b, 0, 0, 0)),
                  pl.BlockSpec((1, HEADS, STEP_ROWS, HEAD_DIM), lambda b, i: (b, 0, i, 0)),
                  tile, pl.BlockSpec((1, 1, 3 * d), lambda b, i: (b, 0, 0)),
                  const(d, d), const(1, d)],
        out_specs=tile,
        out_shape=jax.ShapeDtypeStruct((bsz, t, d), F32),
        scratch_shapes=[
            *[pltpu.VMEM((nt, MOBA_BLOCK, TILE), F32)] * (LOOKAHEAD + 1),
            pltpu.VMEM((HEADS, STEP_ROWS, HEAD_DIM), BF16),
        ],
        compiler_params=pltpu.CompilerParams(
            dimension_semantics=("arbitrary", "arbitrary"), vmem_limit_bytes=VMEM_LIMIT),
        name="moba_attention",
    )(q, k, vt, kmean, z, h, mod.reshape(bsz, 1, 3 * d), w_out, post_g.reshape(1, d))


def kernel(x, c, positions, mod_w, mod_b, pre_norm_g, post_norm_g, a_w_in, a_w_out, a_out_norm_g,
           a_lb_logits, kv_norm_g, w_kv, b_w_in, b_w_out):
    assert x.shape[1] % STEP_ROWS == 0 and STEP_ROWS % TILE == 0 and x.shape[2] == HEADS * HEAD_DIM
    assert mod_w.shape[0] == 2 and a_w_in.shape[0] == 1 and b_w_in.shape[0] == 1
    mod = _modulation(c, mod_w, mod_b)
    cosf, sinf = _rope_tables(positions)
    h1 = _layer0(x, mod[0], pre_norm_g[0], a_w_in[0].astype(BF16), a_w_out[0].astype(BF16),
                 a_out_norm_g[0], a_lb_logits, post_norm_g[0])
    d = x.shape[2]
    k, vt, q, z, kmean = _proj1(h1, mod[1], kv_norm_g, pre_norm_g[1], w_kv[:, :d].astype(BF16),
                                w_kv[:, d:].astype(BF16).T, b_w_in[0].astype(BF16), cosf, sinf)
    kmean = jnp.transpose(kmean, (0, 2, 1, 3))
    return _moba(q, k, vt, kmean, z, h1, mod[1], b_w_out[0].astype(BF16), post_norm_g[1])
```

```python
import math

import jax
import jax.numpy as jnp
from jax import lax
from jax.experimental import pallas as pl
from jax.experimental.pallas import tpu as pltpu

F32 = jnp.float32
BF16 = jnp.bfloat16

NORM_EPS = 1e-6
ROPE_THETA = 10000.0
HEADS = 8
HEAD_DIM = 128
SUBLANES = 8
BF16_ROWS = 16
MOBA_BLOCK = 256
MOBA_TOPK = 3
TILE = 256
STEP_ROWS = 512
BASE = 8
EXP2_CLAMP = 115.0
NEG = -1e30
LOG2E = 1.4426950408889634
VT_ROWS = HEAD_DIM + BF16_ROWS
LOOKAHEAD = 3
VMEM_LIMIT = 56 * 1024 * 1024


def _nt(a, b):
    return lax.dot_general(a, b, (((1,), (1,)), ((), ())), preferred_element_type=F32)


def _tn(a, b):
    return lax.dot_general(a, b, (((0,), (0,)), ((), ())), preferred_element_type=F32)


def _dot(a, b):
    return jnp.dot(a, b, preferred_element_type=F32)


def _split(x):
    hi = x.astype(BF16)
    lo = (x - hi.astype(F32)).astype(BF16)
    return hi, lo


def _sigmoid(x):
    return 1.0 / (1.0 + jnp.exp2(x * (-LOG2E)))


def _silu(x):
    return x * _sigmoid(x)


def _rms(x):
    return x * lax.rsqrt(jnp.mean(x * x, axis=-1, keepdims=True) + NORM_EPS)


def _mod_kernel(c_ref, w_ref, b_ref, o_ref):
    a = _silu(c_ref[...])
    ah, al = _split(a)
    wh, wl = _split(w_ref[0])
    o_ref[0] = _dot(ah, wh) + (_dot(ah, wl) + _dot(al, wh)) + b_ref[0]


def _modulation(c, mod_w, mod_b):
    depth, d, d3 = mod_w.shape
    bsz = c.shape[0]
    nb = d3 // d
    return pl.pallas_call(
        _mod_kernel,
        grid=(depth, nb),
        in_specs=[
            pl.BlockSpec((bsz, d), lambda l, j: (0, 0)),
            pl.BlockSpec((1, d, d), lambda l, j: (l, 0, j)),
            pl.BlockSpec((1, 1, d), lambda l, j: (l, 0, j)),
        ],
        out_specs=pl.BlockSpec((1, bsz, d), lambda l, j: (l, 0, j)),
        out_shape=jax.ShapeDtypeStruct((depth, bsz, d3), F32),
        compiler_params=pltpu.CompilerParams(vmem_limit_bytes=VMEM_LIMIT),
        name="modulation",
    )(c, mod_w, mod_b.reshape(depth, 1, d3))


def _rope_kernel(pos_ref, cos_ref, sin_ref):
    t2 = pos_ref.shape[1] // 2
    lane = lax.broadcasted_iota(jnp.int32, (1, HEAD_DIM), 1)
    half = HEAD_DIM // 2
    low = lane < half
    i = (lane & (half - 1)).astype(F32)
    inv = jnp.exp(i * (-2.0 / HEAD_DIM * math.log(ROPE_THETA)))
    pos_a = pos_ref[0, 0:t2].astype(F32)
    pos_b = pos_ref[0, t2:2 * t2].astype(F32)
    ang = jnp.where(low, pos_a, pos_b) * inv
    c = jnp.cos(ang)
    s = jnp.sin(ang)
    cr = pltpu.roll(c, half, 1)
    sr = pltpu.roll(s, half, 1)
    cos_ref[0, 0:t2] = jnp.where(low, c, cr)
    cos_ref[0, t2:2 * t2] = jnp.where(low, cr, c)
    sin_ref[0, 0:t2] = jnp.where(low, -s, sr)
    sin_ref[0, t2:2 * t2] = jnp.where(low, -sr, s)


def _rope_tables(positions):
    bsz, t = positions.shape
    out = jax.ShapeDtypeStruct((bsz, t, HEAD_DIM), F32)
    return pl.pallas_call(
        _rope_kernel,
        grid=(bsz,),
        in_specs=[pl.BlockSpec((1, t, 1), lambda b: (b, 0, 0))],
        out_specs=[pl.BlockSpec((1, t, HEAD_DIM), lambda b: (b, 0, 0))] * 2,
        out_shape=[out, out],
        compiler_params=pltpu.CompilerParams(vmem_limit_bytes=VMEM_LIMIT),
        name="rope_tables",
    )(positions.reshape(bsz, t, 1))


def _rope(x, cosf, sinf):
    return x * cosf + pltpu.roll(x, HEAD_DIM // 2, 1) * sinf


def _base_mask():
    r = lax.broadcasted_iota(jnp.int32, (TILE // 2, TILE // 2), 0)
    c = lax.broadcasted_iota(jnp.int32, (TILE // 2, TILE // 2), 1)
    return ((r ^ c) < BASE) & (c <= r)


def _block_masks():
    r = lax.broadcasted_iota(jnp.int32, (TILE // 2, TILE // 2), 0)
    c = lax.broadcasted_iota(jnp.int32, (TILE // 2, TILE // 2), 1)
    x = r ^ c
    masks = {}
    m = BASE
    while m < TILE // 2:
        masks[m] = x < m
        m *= 2
    return masks


def _hgrn2_scores(q, k, v, b, st, base_mask, block_masks):
    half = TILE // 2
    vb = v.astype(BF16)
    nb = TILE // BASE
    b3 = b.reshape(nb, BASE, HEAD_DIM)
    e = (b3 - b3[:, BASE // 2 - 1:BASE // 2, :]).reshape(TILE, HEAD_DIM)
    xq = jnp.exp2(jnp.minimum(e, EXP2_CLAMP))
    xk = jnp.exp2(jnp.minimum(-e, EXP2_CLAMP))
    qx, kx = (q * xq).astype(BF16), (k * xk).astype(BF16)
    base = [jnp.where(base_mask, _nt(qx[r], kx[r]), 0.0).astype(BF16)
            for r in (slice(0, half), slice(half, TILE))]
    probs = []
    m = BASE
    while m < TILE:
        nb = TILE // (2 * m)
        split = lambda a: a.reshape(nb, 2 * m, HEAD_DIM)
        b3 = split(b)
        ref = b3[:, m - 1:m, :]
        qc = (split(q)[:, m:] * jnp.exp2(b3[:, m:] - ref)).reshape(half, HEAD_DIM).astype(BF16)
        kc = (split(k)[:, :m] * jnp.exp2(ref - b3[:, :m])).reshape(half, HEAD_DIM).astype(BF16)
        p = _nt(qc, kc)
        if m in block_masks:
            p = jnp.where(block_masks[m], p, 0.0)
        probs.append(p.astype(BF16))
        m *= 2
    o_state = _nt((q * jnp.exp2(b)).astype(BF16), st.astype(BF16))
    b_end = b[TILE - 1:TILE, :]
    kw = (k * jnp.exp2(b_end - b)).astype(BF16)
    st_new = st * jnp.exp2(b_end) + _tn(vb, kw)
    return base, probs, o_state, st_new


def _hgrn2_values(base, probs, o_state, v):
    half = TILE // 2
    vb = v.astype(BF16)
    o = o_state + jnp.concatenate([_dot(base[0], vb[0:half]), _dot(base[1], vb[half:TILE])], axis=0)
    rows = [o[r:r + SUBLANES] for r in range(0, TILE, SUBLANES)]
    m = BASE
    for p in probs:
        nb = TILE // (2 * m)
        src = vb if m % BF16_ROWS == 0 else v
        vc = src.reshape(nb, 2 * m, HEAD_DIM)[:, :m].reshape(half, HEAD_DIM).astype(BF16)
        oc = _dot(p, vc)
        for j in range(0, half, SUBLANES):
            t = (j // m) * 2 * m + m + j % m
            rows[t // SUBLANES] = rows[t // SUBLANES] + oc[j:j + SUBLANES]
        m *= 2
    return jnp.concatenate(rows, axis=0)


def _layer0_kernel(x_ref, mod_ref, png_ref, win_ref, wout_ref, ong_ref, lbl_ref, pong_ref,
                   o_ref, st_ref, og_ref):
    @pl.when(pl.program_id(1) == 0)
    def _():
        st_ref[...] = jnp.zeros_like(st_ref)

    d = x_ref.shape[-1]
    x = x_ref[0]
    mod = mod_ref[0]
    shift, scale, gate = mod[:, :d], mod[:, d:2 * d], mod[:, 2 * d:]
    u = (_rms(x) * png_ref[...]) * (1.0 + scale) + shift
    ub = u.astype(BF16)

    lbl = lbl_ref[...]
    ex = jnp.exp(lbl - jnp.max(lbl, axis=0, keepdims=True))
    lb = ex[0:1, :] / jnp.sum(ex, axis=0, keepdims=True)

    fg = lb + (1.0 - lb) * _sigmoid(_dot(ub, win_ref[:, d:2 * d]))
    qa = _silu(_dot(ub, win_ref[:, 0:d]))
    va = _dot(ub, win_ref[:, 2 * d:3 * d])
    ga = _silu(_dot(ub, win_ref[:, 3 * d:4 * d])) * jnp.concatenate([ong_ref[...]] * HEADS, axis=1)
    ka = 1.0 - fg
    lf_hi, lf_lo = _split(jnp.log2(fg))
    r = lax.broadcasted_iota(jnp.int32, (TILE, TILE), 0)
    c = lax.broadcasted_iota(jnp.int32, (TILE, TILE), 1)
    tri = (c <= r).astype(BF16)
    chunks = [slice(j * TILE, (j + 1) * TILE) for j in range(x.shape[0] // TILE)]
    bcum = [_dot(tri, lf_hi[rows]) + _dot(tri, lf_lo[rows]) for rows in chunks]

    base_mask, block_masks = _base_mask(), _block_masks()
    head = lambda h: slice(h * HEAD_DIM, (h + 1) * HEAD_DIM)
    units = [(j, h) for j in range(len(chunks)) for h in range(HEADS)]

    def scores(unit):
        j, h = unit
        rows, hs = chunks[j], head(h)
        base, probs, o_state, st_new = _hgrn2_scores(qa[rows, hs], ka[rows, hs], va[rows, hs], bcum[j][:, hs],
                                                     st_ref[h], base_mask, block_masks)
        st_ref[h] = st_new
        return base, probs, o_state

    staged = scores(units[0])
    for n, (j, h) in enumerate(units):
        staged_next = scores(units[n + 1]) if n + 1 < len(units) else None
        rows, hs = chunks[j], head(h)
        o = _hgrn2_values(*staged, va[rows, hs])
        og_ref[rows, hs] = (_rms(o) * ga[rows, hs]).astype(BF16)
        staged = staged_next

    y = _dot(og_ref[...], wout_ref[...])
    o_ref[0] = x + gate * (_rms(y) * pong_ref[...])


def _layer0(x, mod, pre_g, w_in, w_out, out_g, lb_logits, post_g):
    bsz, t, d = x.shape
    nt = t // STEP_ROWS
    const = lambda *shape: pl.BlockSpec(shape, lambda b, i: (0,) * len(shape))
    return pl.pallas_call(
        _layer0_kernel,
        grid=(bsz, nt),
        in_specs=[
            pl.BlockSpec((1, STEP_ROWS, d), lambda b, i: (b, i, 0)),
            pl.BlockSpec((1, 1, 3 * d), lambda b, i: (b, 0, 0)),
            const(1, d),
            const(d, 4 * d),
            const(d, d),
            const(1, HEAD_DIM),
            const(lb_logits.shape[0], d),
            const(1, d),
        ],
        out_specs=pl.BlockSpec((1, STEP_ROWS, d), lambda b, i: (b, i, 0)),
        out_shape=jax.ShapeDtypeStruct((bsz, t, d), F32),
        scratch_shapes=[
            pltpu.VMEM((HEADS, HEAD_DIM, HEAD_DIM), F32),
            pltpu.VMEM((STEP_ROWS, d), BF16),
        ],
        compiler_params=pltpu.CompilerParams(
            dimension_semantics=("arbitrary", "arbitrary"), vmem_limit_bytes=VMEM_LIMIT),
        name="hgrn2_layer",
    )(x, mod.reshape(bsz, 1, 3 * d), pre_g.reshape(1, d), w_in, w_out,
      out_g.reshape(1, HEAD_DIM), lb_logits, post_g.reshape(1, d))


def _proj1_kernel(h_ref, mod_ref, kvg_ref, png_ref, wk_ref, wvt_ref, win_ref, cos_ref, sin_ref,
                  k_ref, vt_ref, q_ref, z_ref, km_ref):
    d = h_ref.shape[-1]
    xn = _rms(h_ref[0])
    mod = mod_ref[0]
    shift, scale = mod[:, :d], mod[:, d:2 * d]
    a = (xn * kvg_ref[...]).astype(BF16)
    u = ((xn * png_ref[...]) * (1.0 + scale) + shift).astype(BF16)
    cosf, sinf = cos_ref[0], sin_ref[0]
    kk = _dot(a, wk_ref[...])
    qq = _dot(u, win_ref[:, 0:d])
    qscale = HEAD_DIM ** -0.5 * LOG2E
    vt = _nt(wvt_ref[...], a).astype(BF16)
    rows = h_ref.shape[1]
    ones = jnp.ones((VT_ROWS - HEAD_DIM, rows), BF16)
    for h in range(HEADS):
        hs = slice(h * HEAD_DIM, (h + 1) * HEAD_DIM)
        kr = _rope(kk[:, hs], cosf, sinf)
        k_ref[0, h] = kr.astype(BF16)
        for j in range(rows // MOBA_BLOCK):
            km_ref[0, j, h:h + 1, :] = jnp.mean(kr[j * MOBA_BLOCK:(j + 1) * MOBA_BLOCK], axis=0, keepdims=True)
        q_ref[0, h] = (_rope(qq[:, hs], cosf, sinf) * qscale).astype(BF16)
        vt_ref[0, h, 0:HEAD_DIM, :] = vt[hs, :]
        vt_ref[0, h, HEAD_DIM:VT_ROWS, :] = ones
    zz = _dot(u, win_ref[:, d:2 * d]).astype(BF16)
    for h in range(HEADS):
        z_ref[0, h] = zz[:, h * HEAD_DIM:(h + 1) * HEAD_DIM]


def _proj1(h, mod, kv_g, pre_g, w_k, w_vt, w_in, cosf, sinf):
    bsz, t, d = h.shape
    nt = t // STEP_ROWS
    const = lambda *shape: pl.BlockSpec(shape, lambda b, i: (0,) * len(shape))
    tile = pl.BlockSpec((1, STEP_ROWS, d), lambda b, i: (b, i, 0))
    rope = pl.BlockSpec((1, STEP_ROWS, HEAD_DIM), lambda b, i: (b, i, 0))
    heads = pl.BlockSpec((1, HEADS, STEP_ROWS, HEAD_DIM), lambda b, i: (b, 0, i, 0))
    heads_shape = jax.ShapeDtypeStruct((bsz, HEADS, t, HEAD_DIM), BF16)
    return pl.pallas_call(
        _proj1_kernel,
        grid=(bsz, nt),
        in_specs=[tile, pl.BlockSpec((1, 1, 3 * d), lambda b, i: (b, 0, 0)),
                  const(1, d), const(1, d), const(d, d), const(d, d), const(d, 2 * d), rope, rope],
        out_specs=[heads, pl.BlockSpec((1, HEADS, VT_ROWS, STEP_ROWS), lambda b, i: (b, 0, 0, i)), heads, heads,
                   pl.BlockSpec((1, STEP_ROWS // MOBA_BLOCK, HEADS, HEAD_DIM), lambda b, i: (b, i, 0, 0))],
        out_shape=[heads_shape, jax.ShapeDtypeStruct((bsz, HEADS, VT_ROWS, t), BF16), heads_shape, heads_shape,
                   jax.ShapeDtypeStruct((bsz, t // MOBA_BLOCK, HEADS, HEAD_DIM), F32)],
        compiler_params=pltpu.CompilerParams(
            dimension_semantics=("arbitrary", "arbitrary"), vmem_limit_bytes=VMEM_LIMIT),
        name="moba_proj",
    )(h, mod.reshape(bsz, 1, 3 * d), kv_g.reshape(1, d), pre_g.reshape(1, d), w_k, w_vt, w_in, cosf, sinf)


def _moba_scores(h, rows, nb, q_ref, k_ref, km_ref, s_ref, causal, blk):
    npast = nb - 1
    groups = (MOBA_BLOCK // SUBLANES, SUBLANES, TILE)
    q = q_ref[0, h, rows, :]
    keys = k_ref[0, h, 0:nb * MOBA_BLOCK, :]
    bias = None
    off = 0
    if npast > MOBA_TOPK:
        km = km_ref[0, :, h, :]
        km_hi = km.astype(BF16).astype(F32)
        keys = jnp.concatenate([jnp.concatenate([km_hi, km - km_hi], axis=0).astype(BF16), keys], axis=0)
        off = 2 * km.shape[0]
    s_all = _nt(keys, q)
    if off:
        g = s_all[0:off // 2] + s_all[off // 2:off]
        g = jnp.where(blk < npast, g, -jnp.inf)
        rank = jnp.zeros(g.shape, jnp.int32)
        for m in range(npast):
            gm = g[m:m + 1, :]
            rank = rank + ((gm > g) | ((gm == g) & (m < blk))).astype(jnp.int32)
        bias = jnp.where(rank < MOBA_TOPK, 0.0, NEG)

    m8 = None
    for n in range(nb):
        s = s_all[off + n * MOBA_BLOCK:off + (n + 1) * MOBA_BLOCK]
        if n == npast:
            s = jnp.where(causal, s, NEG)
        elif bias is not None:
            s = s + bias[n:n + 1, :]
        s_ref[n] = s
        mb = jnp.max(s.reshape(groups), axis=0)
        m8 = mb if m8 is None else jnp.maximum(m8, mb)
    return jnp.max(m8, axis=0, keepdims=True)


def _moba_values(h, rows, nb, m, vt_ref, z_ref, s_ref, og_ref):
    p = jnp.concatenate([jnp.exp2(s_ref[n] - m).astype(BF16) for n in range(nb)], axis=0)
    acc = _dot(vt_ref[0, h, :, 0:nb * MOBA_BLOCK], p)
    o = (acc[0:HEAD_DIM] * (1.0 / acc[HEAD_DIM:HEAD_DIM + 1])).T
    og_ref[h, rows, :] = (o * _silu(z_ref[0, h, rows, :].astype(F32))).astype(BF16)


def _moba_kernel(q_ref, k_ref, vt_ref, km_ref, z_ref, h_ref, mod_ref, wout_ref, pong_ref,
                 o_ref, *scratch):
    bufs, og_ref = scratch[:-1], scratch[-1]
    d = h_ref.shape[-1]
    i = pl.program_id(1)
    nblk = km_ref.shape[1]
    tiles = h_ref.shape[1] // TILE
    key = lax.broadcasted_iota(jnp.int32, (MOBA_BLOCK, TILE), 0)
    qry = lax.broadcasted_iota(jnp.int32, (MOBA_BLOCK, TILE), 1)
    causal = key <= qry
    blk = lax.broadcasted_iota(jnp.int32, (nblk, TILE), 0)

    for step in range(nblk // tiles):
        @pl.when(i == step)
        def _(step=step):
            units = [(t, h) for t in range(tiles) for h in range(HEADS)]

            def scores(n):
                t, h = units[n]
                return _moba_scores(h, slice(t * TILE, (t + 1) * TILE), step * tiles + t + 1, q_ref, k_ref, km_ref,
                                    bufs[n % len(bufs)], causal, blk)

            def values(n, m):
                t, h = units[n]
                _moba_values(h, slice(t * TILE, (t + 1) * TILE), step * tiles + t + 1, m, vt_ref, z_ref,
                             bufs[n % len(bufs)], og_ref)

            maxes = {n: scores(n) for n in range(LOOKAHEAD)}
            for n in range(len(units)):
                if n + LOOKAHEAD < len(units):
                    maxes[n + LOOKAHEAD] = scores(n + LOOKAHEAD)
                values(n, maxes.pop(n))

    y = _dot(jnp.concatenate([og_ref[h] for h in range(HEADS)], axis=1), wout_ref[...])
    gate = mod_ref[0][:, 2 * d:]
    o_ref[0] = h_ref[0] + gate * (_rms(y) * pong_ref[...])


def _moba(q, k, vt, kmean, z, h, mod, w_out, post_g):
    bsz, t, d = h.shape
    nt = t // TILE
    const = lambda *shape: pl.BlockSpec(shape, lambda b, i: (0,) * len(shape))
    tile = pl.BlockSpec((1, STEP_ROWS, d), lambda b, i: (b, i, 0))
    return pl.pallas_call(
        _moba_kernel,
        grid=(bsz, t // STEP_ROWS),
        in_specs=[pl.BlockSpec((1, HEADS, STEP_ROWS, HEAD_DIM), lambda b, i: (b, 0, i, 0)),
                  pl.BlockSpec((1, HEADS, t, HEAD_DIM), lambda b, i: (b, 0, 0, 0)),
                  pl.BlockSpec((1, HEADS, VT_ROWS, t), lambda b, i: (b, 0, 0, 0)),
                  pl.BlockSpec((1, nt, HEADS, HEAD_DIM), lambda b, i: (b, 0, 0, 0)),
                  pl.BlockSpec((1, HEADS, STEP_ROWS, HEAD_DIM), lambda b, i: (b, 0, i, 0)),
                  tile, pl.BlockSpec((1, 1, 3 * d), lambda b, i: (b, 0, 0)),
                  const(d, d), const(1, d)],
        out_specs=tile,
        out_shape=jax.ShapeDtypeStruct((bsz, t, d), F32),
        scratch_shapes=[
            *[pltpu.VMEM((nt, MOBA_BLOCK, TILE), F32)] * (LOOKAHEAD + 1),
            pltpu.VMEM((HEADS, STEP_ROWS, HEAD_DIM), BF16),
        ],
        compiler_params=pltpu.CompilerParams(
            dimension_semantics=("arbitrary", "arbitrary"), vmem_limit_bytes=VMEM_LIMIT),
        name="moba_attention",
    )(q, k, vt, kmean, z, h, mod.reshape(bsz, 1, 3 * d), w_out, post_g.reshape(1, d))


def kernel(x, c, positions, mod_w, mod_b, pre_norm_g, post_norm_g, a_w_in, a_w_out, a_out_norm_g,
           a_lb_logits, kv_norm_g, w_kv, b_w_in, b_w_out):
    assert x.shape[1] % STEP_ROWS == 0 and STEP_ROWS % TILE == 0 and x.shape[2] == HEADS * HEAD_DIM
    assert mod_w.shape[0] == 2 and a_w_in.shape[0] == 1 and b_w_in.shape[0] == 1
    mod = _modulation(c, mod_w, mod_b)
    cosf, sinf = _rope_tables(positions)
    h1 = _layer0(x, mod[0], pre_norm_g[0], a_w_in[0].astype(BF16), a_w_out[0].astype(BF16),
                 a_out_norm_g[0], a_lb_logits, post_norm_g[0])
    d = x.shape[2]
    k, vt, q, z, kmean = _proj1(h1, mod[1], kv_norm_g, pre_norm_g[1], w_kv[:, :d].astype(BF16),
                                w_kv[:, d:].astype(BF16).T, b_w_in[0].astype(BF16), cosf, sinf)
    return _moba(q, k, vt, kmean, z, h1, mod[1], b_w_out[0].astype(BF16), post_norm_g[1])
```

```python
import math

import jax
import jax.numpy as jnp
from jax import lax
from jax.experimental import pallas as pl
from jax.experimental.pallas import tpu as pltpu

F32 = jnp.float32
BF16 = jnp.bfloat16

NORM_EPS = 1e-6
ROPE_THETA = 10000.0
HEADS = 8
HEAD_DIM = 128
SUBLANES = 8
BF16_ROWS = 16
MOBA_BLOCK = 256
MOBA_TOPK = 3
TILE = 256
STEP_ROWS = 512
BASE = 8
EXP2_CLAMP = 115.0
NEG = -1e30
LOG2E = 1.4426950408889634
VT_ROWS = HEAD_DIM + BF16_ROWS
LOOKAHEAD = 3
VMEM_LIMIT = 56 * 1024 * 1024


def _nt(a, b):
    return lax.dot_general(a, b, (((1,), (1,)), ((), ())), preferred_element_type=F32)


def _tn(a, b):
    return lax.dot_general(a, b, (((0,), (0,)), ((), ())), preferred_element_type=F32)


def _dot(a, b):
    return jnp.dot(a, b, preferred_element_type=F32)


def _split(x):
    hi = x.astype(BF16)
    lo = (x - hi.astype(F32)).astype(BF16)
    return hi, lo


def _sigmoid(x):
    return 1.0 / (1.0 + jnp.exp2(x * (-LOG2E)))


def _silu(x):
    return x * _sigmoid(x)


def _rms(x):
    return x * lax.rsqrt(jnp.mean(x * x, axis=-1, keepdims=True) + NORM_EPS)


def _mod_kernel(c_ref, w_ref, b_ref, o_ref):
    a = _silu(c_ref[...])
    ah, al = _split(a)
    wh, wl = _split(w_ref[0])
    o_ref[0] = _dot(ah, wh) + (_dot(ah, wl) + _dot(al, wh)) + b_ref[0]


def _modulation(c, mod_w, mod_b):
    depth, d, d3 = mod_w.shape
    bsz = c.shape[0]
    nb = d3 // d
    return pl.pallas_call(
        _mod_kernel,
        grid=(depth, nb),
        in_specs=[
            pl.BlockSpec((bsz, d), lambda l, j: (0, 0)),
            pl.BlockSpec((1, d, d), lambda l, j: (l, 0, j)),
            pl.BlockSpec((1, 1, d), lambda l, j: (l, 0, j)),
        ],
        out_specs=pl.BlockSpec((1, bsz, d), lambda l, j: (l, 0, j)),
        out_shape=jax.ShapeDtypeStruct((depth, bsz, d3), F32),
        compiler_params=pltpu.CompilerParams(vmem_limit_bytes=VMEM_LIMIT),
        name="modulation",
    )(c, mod_w, mod_b.reshape(depth, 1, d3))


def _rope_kernel(pos_ref, cos_ref, sin_ref):
    t2 = pos_ref.shape[1] // 2
    lane = lax.broadcasted_iota(jnp.int32, (1, HEAD_DIM), 1)
    half = HEAD_DIM // 2
    low = lane < half
    i = (lane & (half - 1)).astype(F32)
    inv = jnp.exp(i * (-2.0 / HEAD_DIM * math.log(ROPE_THETA)))
    pos_a = pos_ref[0, 0:t2].astype(F32)
    pos_b = pos_ref[0, t2:2 * t2].astype(F32)
    ang = jnp.where(low, pos_a, pos_b) * inv
    c = jnp.cos(ang)
    s = jnp.sin(ang)
    cr = pltpu.roll(c, half, 1)
    sr = pltpu.roll(s, half, 1)
    cos_ref[0, 0:t2] = jnp.where(low, c, cr)
    cos_ref[0, t2:2 * t2] = jnp.where(low, cr, c)
    sin_ref[0, 0:t2] = jnp.where(low, -s, sr)
    sin_ref[0, t2:2 * t2] = jnp.where(low, -sr, s)


def _rope_tables(positions):
    bsz, t = positions.shape
    out = jax.ShapeDtypeStruct((bsz, t, HEAD_DIM), F32)
    return pl.pallas_call(
        _rope_kernel,
        grid=(bsz,),
        in_specs=[pl.BlockSpec((1, t, 1), lambda b: (b, 0, 0))],
        out_specs=[pl.BlockSpec((1, t, HEAD_DIM), lambda b: (b, 0, 0))] * 2,
        out_shape=[out, out],
        compiler_params=pltpu.CompilerParams(vmem_limit_bytes=VMEM_LIMIT),
        name="rope_tables",
    )(positions.reshape(bsz, t, 1))


def _rope(x, cosf, sinf):
    return x * cosf + pltpu.roll(x, HEAD_DIM // 2, 1) * sinf


def _base_mask():
    r = lax.broadcasted_iota(jnp.int32, (TILE // 2, TILE // 2), 0)
    c = lax.broadcasted_iota(jnp.int32, (TILE // 2, TILE // 2), 1)
    return ((r ^ c) < BASE) & (c <= r)


def _block_masks():
    r = lax.broadcasted_iota(jnp.int32, (TILE // 2, TILE // 2), 0)
    c = lax.broadcasted_iota(jnp.int32, (TILE // 2, TILE // 2), 1)
    x = r ^ c
    masks = {}
    m = BASE
    while m < TILE // 2:
        masks[m] = x < m
        m *= 2
    return masks


def _hgrn2_scores(q, k, v, b, st, base_mask, block_masks):
    half = TILE // 2
    vb = v.astype(BF16)
    nb = TILE // BASE
    b3 = b.reshape(nb, BASE, HEAD_DIM)
    e = (b3 - b3[:, BASE // 2 - 1:BASE // 2, :]).reshape(TILE, HEAD_DIM)
    xq = jnp.exp2(jnp.minimum(e, EXP2_CLAMP))
    xk = jnp.exp2(jnp.minimum(-e, EXP2_CLAMP))
    qx, kx = (q * xq).astype(BF16), (k * xk).astype(BF16)
    base = [jnp.where(base_mask, _nt(qx[r], kx[r]), 0.0).astype(BF16)
            for r in (slice(0, half), slice(half, TILE))]
    probs = []
    m = BASE
    while m < TILE:
        nb = TILE // (2 * m)
        split = lambda a: a.reshape(nb, 2 * m, HEAD_DIM)
        b3 = split(b)
        ref = b3[:, m - 1:m, :]
        qc = (split(q)[:, m:] * jnp.exp2(b3[:, m:] - ref)).reshape(half, HEAD_DIM).astype(BF16)
        kc = (split(k)[:, :m] * jnp.exp2(ref - b3[:, :m])).reshape(half, HEAD_DIM).astype(BF16)
        p = _nt(qc, kc)
        if m in block_masks:
            p = jnp.where(block_masks[m], p, 0.0)
        probs.append(p.astype(BF16))
        m *= 2
    o_state = _nt((q * jnp.exp2(b)).astype(BF16), st.astype(BF16))
    b_end = b[TILE - 1:TILE, :]
    kw = (k * jnp.exp2(b_end - b)).astype(BF16)
    st_new = st * jnp.exp2(b_end) + _tn(vb, kw)
    return base, probs, o_state, st_new


def _hgrn2_values(base, probs, o_state, v):
    half = TILE // 2
    vb = v.astype(BF16)
    o = o_state + jnp.concatenate([_dot(base[0], vb[0:half]), _dot(base[1], vb[half:TILE])], axis=0)
    rows = [o[r:r + SUBLANES] for r in range(0, TILE, SUBLANES)]
    m = BASE
    for p in probs:
        nb = TILE // (2 * m)
        src = vb if m % BF16_ROWS == 0 else v
        vc = src.reshape(nb, 2 * m, HEAD_DIM)[:, :m].reshape(half, HEAD_DIM).astype(BF16)
        oc = _dot(p, vc)
        for j in range(0, half, SUBLANES):
            t = (j // m) * 2 * m + m + j % m
            rows[t // SUBLANES] = rows[t // SUBLANES] + oc[j:j + SUBLANES]
        m *= 2
    return jnp.concatenate(rows, axis=0)


def _layer0_kernel(x_ref, mod_ref, png_ref, win_ref, wout_ref, ong_ref, lbl_ref, pong_ref,
                   o_ref, st_ref, og_ref):
    @pl.when(pl.program_id(1) == 0)
    def _():
        st_ref[...] = jnp.zeros_like(st_ref)

    d = x_ref.shape[-1]
    x = x_ref[0]
    mod = mod_ref[0]
    shift, scale, gate = mod[:, :d], mod[:, d:2 * d], mod[:, 2 * d:]
    u = (_rms(x) * png_ref[...]) * (1.0 + scale) + shift
    ub = u.astype(BF16)

    lbl = lbl_ref[...]
    ex = jnp.exp(lbl - jnp.max(lbl, axis=0, keepdims=True))
    lb = ex[0:1, :] / jnp.sum(ex, axis=0, keepdims=True)

    fg = lb + (1.0 - lb) * _sigmoid(_dot(ub, win_ref[:, d:2 * d]))
    qa = _silu(_dot(ub, win_ref[:, 0:d]))
    va = _dot(ub, win_ref[:, 2 * d:3 * d])
    ga = _silu(_dot(ub, win_ref[:, 3 * d:4 * d])) * jnp.concatenate([ong_ref[...]] * HEADS, axis=1)
    ka = 1.0 - fg
    lf_hi, lf_lo = _split(jnp.log2(fg))
    r = lax.broadcasted_iota(jnp.int32, (TILE, TILE), 0)
    c = lax.broadcasted_iota(jnp.int32, (TILE, TILE), 1)
    tri = (c <= r).astype(BF16)
    chunks = [slice(j * TILE, (j + 1) * TILE) for j in range(x.shape[0] // TILE)]
    bcum = [_dot(tri, lf_hi[rows]) + _dot(tri, lf_lo[rows]) for rows in chunks]

    base_mask, block_masks = _base_mask(), _block_masks()
    head = lambda h: slice(h * HEAD_DIM, (h + 1) * HEAD_DIM)
    units = [(j, h) for j in range(len(chunks)) for h in range(HEADS)]

    def scores(unit):
        j, h = unit
        rows, hs = chunks[j], head(h)
        base, probs, o_state, st_new = _hgrn2_scores(qa[rows, hs], ka[rows, hs], va[rows, hs], bcum[j][:, hs],
                                                     st_ref[h], base_mask, block_masks)
        st_ref[h] = st_new
        return base, probs, o_state

    staged = scores(units[0])
    for n, (j, h) in enumerate(units):
        staged_next = scores(units[n + 1]) if n + 1 < len(units) else None
        rows, hs = chunks[j], head(h)
        o = _hgrn2_values(*staged, va[rows, hs])
        og_ref[rows, hs] = (_rms(o) * ga[rows, hs]).astype(BF16)
        staged = staged_next

    y = _dot(og_ref[...], wout_ref[...])
    o_ref[0] = x + gate * (_rms(y) * pong_ref[...])


def _layer0(x, mod, pre_g, w_in, w_out, out_g, lb_logits, post_g):
    bsz, t, d = x.shape
    nt = t // STEP_ROWS
    const = lambda *shape: pl.BlockSpec(shape, lambda b, i: (0,) * len(shape))
    return pl.pallas_call(
        _layer0_kernel,
        grid=(bsz, nt),
        in_specs=[
            pl.BlockSpec((1, STEP_ROWS, d), lambda b, i: (b, i, 0)),
            pl.BlockSpec((1, 1, 3 * d), lambda b, i: (b, 0, 0)),
            const(1, d),
            const(d, 4 * d),
            const(d, d),
            const(1, HEAD_DIM),
            const(lb_logits.shape[0], d),
            const(1, d),
        ],
        out_specs=pl.BlockSpec((1, STEP_ROWS, d), lambda b, i: (b, i, 0)),
        out_shape=jax.ShapeDtypeStruct((bsz, t, d), F32),
        scratch_shapes=[
            pltpu.VMEM((HEADS, HEAD_DIM, HEAD_DIM), F32),
            pltpu.VMEM((STEP_ROWS, d), BF16),
        ],
        compiler_params=pltpu.CompilerParams(
            dimension_semantics=("arbitrary", "arbitrary"), vmem_limit_bytes=VMEM_LIMIT),
        name="hgrn2_layer",
    )(x, mod.reshape(bsz, 1, 3 * d), pre_g.reshape(1, d), w_in, w_out,
      out_g.reshape(1, HEAD_DIM), lb_logits, post_g.reshape(1, d))


def _proj1_kernel(h_ref, mod_ref, kvg_ref, png_ref, wk_ref, wvt_ref, win_ref, cos_ref, sin_ref,
                  k_ref, vt_ref, q_ref, z_ref, km_ref):
    d = h_ref.shape[-1]
    xn = _rms(h_ref[0])
    mod = mod_ref[0]
    shift, scale = mod[:, :d], mod[:, d:2 * d]
    a = (xn * kvg_ref[...]).astype(BF16)
    u = ((xn * png_ref[...]) * (1.0 + scale) + shift).astype(BF16)
    cosf, sinf = cos_ref[0], sin_ref[0]
    kk = _dot(a, wk_ref[...])
    qq = _dot(u, win_ref[:, 0:d])
    qscale = HEAD_DIM ** -0.5 * LOG2E
    vt = _dot(a, wvt_ref[...]).T.astype(BF16)
    rows = h_ref.shape[1]
    ones = jnp.ones((VT_ROWS - HEAD_DIM, rows), BF16)
    for h in range(HEADS):
        hs = slice(h * HEAD_DIM, (h + 1) * HEAD_DIM)
        kr = _rope(kk[:, hs], cosf, sinf)
        k_ref[0, h] = kr.astype(BF16)
        for j in range(rows // MOBA_BLOCK):
            km_ref[0, j, h:h + 1, :] = jnp.mean(kr[j * MOBA_BLOCK:(j + 1) * MOBA_BLOCK], axis=0, keepdims=True)
        q_ref[0, h] = (_rope(qq[:, hs], cosf, sinf) * qscale).astype(BF16)
        vt_ref[0, h, 0:HEAD_DIM, :] = vt[hs, :]
        vt_ref[0, h, HEAD_DIM:VT_ROWS, :] = ones
    zz = _dot(u, win_ref[:, d:2 * d]).astype(BF16)
    for h in range(HEADS):
        z_ref[0, h] = zz[:, h * HEAD_DIM:(h + 1) * HEAD_DIM]


def _proj1(h, mod, kv_g, pre_g, w_k, w_vt, w_in, cosf, sinf):
    bsz, t, d = h.shape
    nt = t // STEP_ROWS
    const = lambda *shape: pl.BlockSpec(shape, lambda b, i: (0,) * len(shape))
    tile = pl.BlockSpec((1, STEP_ROWS, d), lambda b, i: (b, i, 0))
    rope = pl.BlockSpec((1, STEP_ROWS, HEAD_DIM), lambda b, i: (b, i, 0))
    heads = pl.BlockSpec((1, HEADS, STEP_ROWS, HEAD_DIM), lambda b, i: (b, 0, i, 0))
    heads_shape = jax.ShapeDtypeStruct((bsz, HEADS, t, HEAD_DIM), BF16)
    return pl.pallas_call(
        _proj1_kernel,
        grid=(bsz, nt),
        in_specs=[tile, pl.BlockSpec((1, 1, 3 * d), lambda b, i: (b, 0, 0)),
                  const(1, d), const(1, d), const(d, d), const(d, d), const(d, 2 * d), rope, rope],
        out_specs=[heads, pl.BlockSpec((1, HEADS, VT_ROWS, STEP_ROWS), lambda b, i: (b, 0, 0, i)), heads, heads,
                   pl.BlockSpec((1, STEP_ROWS // MOBA_BLOCK, HEADS, HEAD_DIM), lambda b, i: (b, i, 0, 0))],
        out_shape=[heads_shape, jax.ShapeDtypeStruct((bsz, HEADS, VT_ROWS, t), BF16), heads_shape, heads_shape,
                   jax.ShapeDtypeStruct((bsz, t // MOBA_BLOCK, HEADS, HEAD_DIM), F32)],
        compiler_params=pltpu.CompilerParams(
            dimension_semantics=("arbitrary", "arbitrary"), vmem_limit_bytes=VMEM_LIMIT),
        name="moba_proj",
    )(h, mod.reshape(bsz, 1, 3 * d), kv_g.reshape(1, d), pre_g.reshape(1, d), w_k, w_vt, w_in, cosf, sinf)


def _moba_scores(h, rows, nb, q_ref, k_ref, km_ref, s_ref, causal, blk):
    npast = nb - 1
    groups = (MOBA_BLOCK // SUBLANES, SUBLANES, TILE)
    q = q_ref[0, h, rows, :]
    keys = k_ref[0, h, 0:nb * MOBA_BLOCK, :]
    bias = None
    off = 0
    if npast > MOBA_TOPK:
        km = km_ref[0, :, h, :]
        km_hi = km.astype(BF16).astype(F32)
        keys = jnp.concatenate([jnp.concatenate([km_hi, km - km_hi], axis=0).astype(BF16), keys], axis=0)
        off = 2 * km.shape[0]
    s_all = _nt(keys, q)
    if off:
        g = s_all[0:off // 2] + s_all[off // 2:off]
        g = jnp.where(blk < npast, g, -jnp.inf)
        rank = jnp.zeros(g.shape, jnp.int32)
        for m in range(npast):
            gm = g[m:m + 1, :]
            rank = rank + ((gm > g) | ((gm == g) & (m < blk))).astype(jnp.int32)
        bias = jnp.where(rank < MOBA_TOPK, 0.0, NEG)

    m8 = None
    for n in range(nb):
        s = s_all[off + n * MOBA_BLOCK:off + (n + 1) * MOBA_BLOCK]
        if n == npast:
            s = jnp.where(causal, s, NEG)
        elif bias is not None:
            s = s + bias[n:n + 1, :]
        s_ref[n] = s
        mb = jnp.max(s.reshape(groups), axis=0)
        m8 = mb if m8 is None else jnp.maximum(m8, mb)
    return jnp.max(m8, axis=0, keepdims=True)


def _moba_values(h, rows, nb, m, vt_ref, z_ref, s_ref, og_ref):
    p = jnp.concatenate([jnp.exp2(s_ref[n] - m).astype(BF16) for n in range(nb)], axis=0)
    acc = _dot(vt_ref[0, h, :, 0:nb * MOBA_BLOCK], p)
    o = (acc[0:HEAD_DIM] * (1.0 / acc[HEAD_DIM:HEAD_DIM + 1])).T
    og_ref[h, rows, :] = (o * _silu(z_ref[0, h, rows, :].astype(F32))).astype(BF16)


def _moba_kernel(q_ref, k_ref, vt_ref, km_ref, z_ref, h_ref, mod_ref, wout_ref, pong_ref,
                 o_ref, *scratch):
    bufs, og_ref = scratch[:-1], scratch[-1]
    d = h_ref.shape[-1]
    i = pl.program_id(1)
    nblk = km_ref.shape[1]
    tiles = h_ref.shape[1] // TILE
    key = lax.broadcasted_iota(jnp.int32, (MOBA_BLOCK, TILE), 0)
    qry = lax.broadcasted_iota(jnp.int32, (MOBA_BLOCK, TILE), 1)
    causal = key <= qry
    blk = lax.broadcasted_iota(jnp.int32, (nblk, TILE), 0)

    for step in range(nblk // tiles):
        @pl.when(i == step)
        def _(step=step):
            units = [(t, h) for t in range(tiles) for h in range(HEADS)]

            def scores(n):
                t, h = units[n]
                return _moba_scores(h, slice(t * TILE, (t + 1) * TILE), step * tiles + t + 1, q_ref, k_ref, km_ref,
                                    bufs[n % len(bufs)], causal, blk)

            def values(n, m):
                t, h = units[n]
                _moba_values(h, slice(t * TILE, (t + 1) * TILE), step * tiles + t + 1, m, vt_ref, z_ref,
                             bufs[n % len(bufs)], og_ref)

            maxes = {n: scores(n) for n in range(LOOKAHEAD)}
            for n in range(len(units)):
                if n + LOOKAHEAD < len(units):
                    maxes[n + LOOKAHEAD] = scores(n + LOOKAHEAD)
                values(n, maxes.pop(n))

    y = _dot(jnp.concatenate([og_ref[h] for h in range(HEADS)], axis=1), wout_ref[...])
    gate = mod_ref[0][:, 2 * d:]
    o_ref[0] = h_ref[0] + gate * (_rms(y) * pong_ref[...])


def _moba(q, k, vt, kmean, z, h, mod, w_out, post_g):
    bsz, t, d = h.shape
    nt = t // TILE
    const = lambda *shape: pl.BlockSpec(shape, lambda b, i: (0,) * len(shape))
    tile = pl.BlockSpec((1, STEP_ROWS, d), lambda b, i: (b, i, 0))
    return pl.pallas_call(
        _moba_kernel,
        grid=(bsz, t // STEP_ROWS),
        in_specs=[pl.BlockSpec((1, HEADS, STEP_ROWS, HEAD_DIM), lambda b, i: (b, 0, i, 0)),
                  pl.BlockSpec((1, HEADS, t, HEAD_DIM), lambda b, i: (b, 0, 0, 0)),
                  pl.BlockSpec((1, HEADS, VT_ROWS, t), lambda b, i: (b, 0, 0, 0)),
                  pl.BlockSpec((1, nt, HEADS, HEAD_DIM), lambda b, i: (b, 0, 0, 0)),
                  pl.BlockSpec((1, HEADS, STEP_ROWS, HEAD_DIM), lambda b, i: (b, 0, i, 0)),
                  tile, pl.BlockSpec((1, 1, 3 * d), lambda b, i: (b, 0, 0)),
                  const(d, d), const(1, d)],
        out_specs=tile,
        out_shape=jax.ShapeDtypeStruct((bsz, t, d), F32),
        scratch_shapes=[
            *[pltpu.VMEM((nt, MOBA_BLOCK, TILE), F32)] * (LOOKAHEAD + 1),
            pltpu.VMEM((HEADS, STEP_ROWS, HEAD_DIM), BF16),
        ],
        compiler_params=pltpu.CompilerParams(
            dimension_semantics=("arbitrary", "arbitrary"), vmem_limit_bytes=VMEM_LIMIT),
        name="moba_attention",
    )(q, k, vt, kmean, z, h, mod.reshape(bsz, 1, 3 * d), w_out, post_g.reshape(1, d))


def kernel(x, c, positions, mod_w, mod_b, pre_norm_g, post_norm_g, a_w_in, a_w_out, a_out_norm_g,
           a_lb_logits, kv_norm_g, w_kv, b_w_in, b_w_out):
    assert x.shape[1] % STEP_ROWS == 0 and STEP_ROWS % TILE == 0 and x.shape[2] == HEADS * HEAD_DIM
    assert mod_w.shape[0] == 2 and a_w_in.shape[0] == 1 and b_w_in.shape[0] == 1
    mod = _modulation(c, mod_w, mod_b)
    cosf, sinf = _rope_tables(positions)
    h1 = _layer0(x, mod[0], pre_norm_g[0], a_w_in[0].astype(BF16), a_w_out[0].astype(BF16),
                 a_out_norm_g[0], a_lb_logits, post_norm_g[0])
    d = x.shape[2]
    k, vt, q, z, kmean = _proj1(h1, mod[1], kv_norm_g, pre_norm_g[1], w_kv[:, :d].astype(BF16),
                                w_kv[:, d:].astype(BF16), b_w_in[0].astype(BF16), cosf, sinf)
    return _moba(q, k, vt, kmean, z, h1, mod[1], b_w_out[0].astype(BF16), post_norm_g[1])
```

```python
import math

import jax
import jax.numpy as jnp
from jax import lax
from jax.experimental import pallas as pl
from jax.experimental.pallas import tpu as pltpu

F32 = jnp.float32
BF16 = jnp.bfloat16

NORM_EPS = 1e-6
ROPE_THETA = 10000.0
HEADS = 8
HEAD_DIM = 128
SUBLANES = 8
BF16_ROWS = 16
MOBA_BLOCK = 256
MOBA_TOPK = 3
TILE = 256
STEP_ROWS = 512
BASE = 8
EXP2_CLAMP = 115.0
NEG = -1e30
LOG2E = 1.4426950408889634
VT_ROWS = HEAD_DIM + BF16_ROWS
LOOKAHEAD = 3
VMEM_LIMIT = 56 * 1024 * 1024


def _nt(a, b):
    return lax.dot_general(a, b, (((1,), (1,)), ((), ())), preferred_element_type=F32)


def _tn(a, b):
    return lax.dot_general(a, b, (((0,), (0,)), ((), ())), preferred_element_type=F32)


def _dot(a, b):
    return jnp.dot(a, b, preferred_element_type=F32)


def _split(x):
    hi = x.astype(BF16)
    lo = (x - hi.astype(F32)).astype(BF16)
    return hi, lo


def _sigmoid(x):
    return 1.0 / (1.0 + jnp.exp2(x * (-LOG2E)))


def _silu(x):
    return x * _sigmoid(x)


def _rms(x):
    return x * lax.rsqrt(jnp.mean(x * x, axis=-1, keepdims=True) + NORM_EPS)


def _mod_kernel(c_ref, w_ref, b_ref, o_ref):
    a = _silu(c_ref[...])
    ah, al = _split(a)
    wh, wl = _split(w_ref[0])
    o_ref[0] = _dot(ah, wh) + (_dot(ah, wl) + _dot(al, wh)) + b_ref[0]


def _modulation(c, mod_w, mod_b):
    depth, d, d3 = mod_w.shape
    bsz = c.shape[0]
    nb = d3 // d
    return pl.pallas_call(
        _mod_kernel,
        grid=(depth, nb),
        in_specs=[
            pl.BlockSpec((bsz, d), lambda l, j: (0, 0)),
            pl.BlockSpec((1, d, d), lambda l, j: (l, 0, j)),
            pl.BlockSpec((1, 1, d), lambda l, j: (l, 0, j)),
        ],
        out_specs=pl.BlockSpec((1, bsz, d), lambda l, j: (l, 0, j)),
        out_shape=jax.ShapeDtypeStruct((depth, bsz, d3), F32),
        compiler_params=pltpu.CompilerParams(vmem_limit_bytes=VMEM_LIMIT),
        name="modulation",
    )(c, mod_w, mod_b.reshape(depth, 1, d3))


def _rope_kernel(pos_ref, cos_ref, sin_ref):
    t2 = pos_ref.shape[1] // 2
    lane = lax.broadcasted_iota(jnp.int32, (1, HEAD_DIM), 1)
    half = HEAD_DIM // 2
    low = lane < half
    i = (lane & (half - 1)).astype(F32)
    inv = jnp.exp(i * (-2.0 / HEAD_DIM * math.log(ROPE_THETA)))
    pos_a = pos_ref[0, 0:t2].astype(F32)
    pos_b = pos_ref[0, t2:2 * t2].astype(F32)
    ang = jnp.where(low, pos_a, pos_b) * inv
    c = jnp.cos(ang)
    s = jnp.sin(ang)
    cr = pltpu.roll(c, half, 1)
    sr = pltpu.roll(s, half, 1)
    cos_ref[0, 0:t2] = jnp.where(low, c, cr)
    cos_ref[0, t2:2 * t2] = jnp.where(low, cr, c)
    sin_ref[0, 0:t2] = jnp.where(low, -s, sr)
    sin_ref[0, t2:2 * t2] = jnp.where(low, -sr, s)


def _rope_tables(positions):
    bsz, t = positions.shape
    out = jax.ShapeDtypeStruct((bsz, t, HEAD_DIM), F32)
    return pl.pallas_call(
        _rope_kernel,
        grid=(bsz,),
        in_specs=[pl.BlockSpec((1, t, 1), lambda b: (b, 0, 0))],
        out_specs=[pl.BlockSpec((1, t, HEAD_DIM), lambda b: (b, 0, 0))] * 2,
        out_shape=[out, out],
        compiler_params=pltpu.CompilerParams(vmem_limit_bytes=VMEM_LIMIT),
        name="rope_tables",
    )(positions.reshape(bsz, t, 1))


def _rope(x, cosf, sinf):
    return x * cosf + pltpu.roll(x, HEAD_DIM // 2, 1) * sinf


def _base_mask():
    r = lax.broadcasted_iota(jnp.int32, (TILE // 2, TILE // 2), 0)
    c = lax.broadcasted_iota(jnp.int32, (TILE // 2, TILE // 2), 1)
    return ((r ^ c) < BASE) & (c <= r)


def _block_masks():
    r = lax.broadcasted_iota(jnp.int32, (TILE // 2, TILE // 2), 0)
    c = lax.broadcasted_iota(jnp.int32, (TILE // 2, TILE // 2), 1)
    x = r ^ c
    masks = {}
    m = BASE
    while m < TILE // 2:
        masks[m] = x < m
        m *= 2
    return masks


def _hgrn2_scores(q, k, v, b, st, base_mask, block_masks):
    half = TILE // 2
    vb = v.astype(BF16)
    nb = TILE // BASE
    b3 = b.reshape(nb, BASE, HEAD_DIM)
    e = (b3 - b3[:, BASE // 2 - 1:BASE // 2, :]).reshape(TILE, HEAD_DIM)
    xq = jnp.exp2(jnp.minimum(e, EXP2_CLAMP))
    xk = jnp.exp2(jnp.minimum(-e, EXP2_CLAMP))
    qx, kx = (q * xq).astype(BF16), (k * xk).astype(BF16)
    base = [jnp.where(base_mask, _nt(qx[r], kx[r]), 0.0).astype(BF16)
            for r in (slice(0, half), slice(half, TILE))]
    probs = []
    m = BASE
    while m < TILE:
        nb = TILE // (2 * m)
        split = lambda a: a.reshape(nb, 2 * m, HEAD_DIM)
        b3 = split(b)
        ref = b3[:, m - 1:m, :]
        qc = (split(q)[:, m:] * jnp.exp2(b3[:, m:] - ref)).reshape(half, HEAD_DIM).astype(BF16)
        kc = (split(k)[:, :m] * jnp.exp2(ref - b3[:, :m])).reshape(half, HEAD_DIM).astype(BF16)
        p = _nt(qc, kc)
        if m in block_masks:
            p = jnp.where(block_masks[m], p, 0.0)
        probs.append(p.astype(BF16))
        m *= 2
    o_state = _nt((q * jnp.exp2(b)).astype(BF16), st.astype(BF16))
    b_end = b[TILE - 1:TILE, :]
    kw = (k * jnp.exp2(b_end - b)).astype(BF16)
    st_new = st * jnp.exp2(b_end) + _tn(vb, kw)
    return base, probs, o_state, st_new


def _hgrn2_values(base, probs, o_state, v):
    half = TILE // 2
    vb = v.astype(BF16)
    o = o_state + jnp.concatenate([_dot(base[0], vb[0:half]), _dot(base[1], vb[half:TILE])], axis=0)
    rows = [o[r:r + SUBLANES] for r in range(0, TILE, SUBLANES)]
    m = BASE
    for p in probs:
        nb = TILE // (2 * m)
        src = vb if m % BF16_ROWS == 0 else v
        vc = src.reshape(nb, 2 * m, HEAD_DIM)[:, :m].reshape(half, HEAD_DIM).astype(BF16)
        oc = _dot(p, vc)
        for j in range(0, half, SUBLANES):
            t = (j // m) * 2 * m + m + j % m
            rows[t // SUBLANES] = rows[t // SUBLANES] + oc[j:j + SUBLANES]
        m *= 2
    return jnp.concatenate(rows, axis=0)


def _layer0_kernel(x_ref, mod_ref, png_ref, win_ref, wout_ref, ong_ref, lbl_ref, pong_ref,
                   o_ref, st_ref, og_ref):
    @pl.when(pl.program_id(1) == 0)
    def _():
        st_ref[...] = jnp.zeros_like(st_ref)

    d = x_ref.shape[-1]
    x = x_ref[0]
    mod = mod_ref[0]
    shift, scale, gate = mod[:, :d], mod[:, d:2 * d], mod[:, 2 * d:]
    u = (_rms(x) * png_ref[...]) * (1.0 + scale) + shift
    ub = u.astype(BF16)

    lbl = lbl_ref[...]
    ex = jnp.exp(lbl - jnp.max(lbl, axis=0, keepdims=True))
    lb = ex[0:1, :] / jnp.sum(ex, axis=0, keepdims=True)

    r = lax.broadcasted_iota(jnp.int32, (TILE, TILE), 0)
    c = lax.broadcasted_iota(jnp.int32, (TILE, TILE), 1)
    tri = (c <= r).astype(BF16)
    gain = jnp.concatenate([ong_ref[...]] * HEADS, axis=1)
    chunks = [slice(j * TILE, (j + 1) * TILE) for j in range(x.shape[0] // TILE)]

    def projection_stages(rows, act):
        uc = ub[rows]

        def forget():
            fg = lb + (1.0 - lb) * _sigmoid(_dot(uc, win_ref[:, d:2 * d]))
            act["k"] = 1.0 - fg
            act["lf"] = _split(jnp.log2(fg))

        def query():
            act["q"] = _silu(_dot(uc, win_ref[:, 0:d]))

        def value():
            act["v"] = _dot(uc, win_ref[:, 2 * d:3 * d])

        def out_gate():
            act["g"] = _silu(_dot(uc, win_ref[:, 3 * d:4 * d])) * gain

        def decay():
            act["b"] = _dot(tri, act["lf"][0]) + _dot(tri, act["lf"][1])

        return [forget, query, value, out_gate, decay]

    acts = [dict() for _ in chunks]
    for stage in projection_stages(chunks[0], acts[0]):
        stage()

    base_mask, block_masks = _base_mask(), _block_masks()
    head = lambda h: slice(h * HEAD_DIM, (h + 1) * HEAD_DIM)
    units = [(j, h) for j in range(len(chunks)) for h in range(HEADS)]

    def scores(unit):
        j, h = unit
        a, hs = acts[j], head(h)
        base, probs, o_state, st_new = _hgrn2_scores(a["q"][:, hs], a["k"][:, hs], a["v"][:, hs], a["b"][:, hs],
                                                     st_ref[h], base_mask, block_masks)
        st_ref[h] = st_new
        return base, probs, o_state

    staged = scores(units[0])
    pending = []
    for n, (j, h) in enumerate(units):
        if h == 0 and j + 1 < len(chunks):
            pending = projection_stages(chunks[j + 1], acts[j + 1])
        staged_next = scores(units[n + 1]) if n + 1 < len(units) and (units[n + 1][0] == j or not pending) else None
        rows, hs = chunks[j], head(h)
        o = _hgrn2_values(*staged, acts[j]["v"][:, hs])
        og_ref[rows, hs] = (_rms(o) * acts[j]["g"][:, hs]).astype(BF16)
        if pending and h < HEADS - 1:
            pending.pop(0)()
        if h == HEADS - 1:
            while pending:
                pending.pop(0)()
            if staged_next is None and n + 1 < len(units):
                staged_next = scores(units[n + 1])
        staged = staged_next

    y = _dot(og_ref[...], wout_ref[...])
    o_ref[0] = x + gate * (_rms(y) * pong_ref[...])


def _layer0(x, mod, pre_g, w_in, w_out, out_g, lb_logits, post_g):
    bsz, t, d = x.shape
    nt = t // STEP_ROWS
    const = lambda *shape: pl.BlockSpec(shape, lambda b, i: (0,) * len(shape))
    return pl.pallas_call(
        _layer0_kernel,
        grid=(bsz, nt),
        in_specs=[
            pl.BlockSpec((1, STEP_ROWS, d), lambda b, i: (b, i, 0)),
            pl.BlockSpec((1, 1, 3 * d), lambda b, i: (b, 0, 0)),
            const(1, d),
            const(d, 4 * d),
            const(d, d),
            const(1, HEAD_DIM),
            const(lb_logits.shape[0], d),
            const(1, d),
        ],
        out_specs=pl.BlockSpec((1, STEP_ROWS, d), lambda b, i: (b, i, 0)),
        out_shape=jax.ShapeDtypeStruct((bsz, t, d), F32),
        scratch_shapes=[
            pltpu.VMEM((HEADS, HEAD_DIM, HEAD_DIM), F32),
            pltpu.VMEM((STEP_ROWS, d), BF16),
        ],
        compiler_params=pltpu.CompilerParams(
            dimension_semantics=("arbitrary", "arbitrary"), vmem_limit_bytes=VMEM_LIMIT),
        name="hgrn2_layer",
    )(x, mod.reshape(bsz, 1, 3 * d), pre_g.reshape(1, d), w_in, w_out,
      out_g.reshape(1, HEAD_DIM), lb_logits, post_g.reshape(1, d))


def _proj1_kernel(h_ref, mod_ref, kvg_ref, png_ref, wk_ref, wvt_ref, win_ref, cos_ref, sin_ref,
                  k_ref, vt_ref, q_ref, z_ref, km_ref):
    d = h_ref.shape[-1]
    xn = _rms(h_ref[0])
    mod = mod_ref[0]
    shift, scale = mod[:, :d], mod[:, d:2 * d]
    a = (xn * kvg_ref[...]).astype(BF16)
    u = ((xn * png_ref[...]) * (1.0 + scale) + shift).astype(BF16)
    cosf, sinf = cos_ref[0], sin_ref[0]
    kk = _dot(a, wk_ref[...])
    qq = _dot(u, win_ref[:, 0:d])
    qscale = HEAD_DIM ** -0.5 * LOG2E
    vt = _nt(wvt_ref[...], a).astype(BF16)
    rows = h_ref.shape[1]
    ones = jnp.ones((VT_ROWS - HEAD_DIM, rows), BF16)
    for h in range(HEADS):
        hs = slice(h * HEAD_DIM, (h + 1) * HEAD_DIM)
        kr = _rope(kk[:, hs], cosf, sinf)
        k_ref[0, h] = kr.astype(BF16)
        for j in range(rows // MOBA_BLOCK):
            km_ref[0, j, h:h + 1, :] = jnp.mean(kr[j * MOBA_BLOCK:(j + 1) * MOBA_BLOCK], axis=0, keepdims=True)
        q_ref[0, h] = (_rope(qq[:, hs], cosf, sinf) * qscale).astype(BF16)
        vt_ref[0, h, 0:HEAD_DIM, :] = vt[hs, :]
        vt_ref[0, h, HEAD_DIM:VT_ROWS, :] = ones
    zz = _dot(u, win_ref[:, d:2 * d]).astype(BF16)
    for h in range(HEADS):
        z_ref[0, h] = zz[:, h * HEAD_DIM:(h + 1) * HEAD_DIM]


def _proj1(h, mod, kv_g, pre_g, w_k, w_vt, w_in, cosf, sinf):
    bsz, t, d = h.shape
    nt = t // STEP_ROWS
    const = lambda *shape: pl.BlockSpec(shape, lambda b, i: (0,) * len(shape))
    tile = pl.BlockSpec((1, STEP_ROWS, d), lambda b, i: (b, i, 0))
    rope = pl.BlockSpec((1, STEP_ROWS, HEAD_DIM), lambda b, i: (b, i, 0))
    heads = pl.BlockSpec((1, HEADS, STEP_ROWS, HEAD_DIM), lambda b, i: (b, 0, i, 0))
    heads_shape = jax.ShapeDtypeStruct((bsz, HEADS, t, HEAD_DIM), BF16)
    return pl.pallas_call(
        _proj1_kernel,
        grid=(bsz, nt),
        in_specs=[tile, pl.BlockSpec((1, 1, 3 * d), lambda b, i: (b, 0, 0)),
                  const(1, d), const(1, d), const(d, d), const(d, d), const(d, 2 * d), rope, rope],
        out_specs=[heads, pl.BlockSpec((1, HEADS, VT_ROWS, STEP_ROWS), lambda b, i: (b, 0, 0, i)), heads, heads,
                   pl.BlockSpec((1, STEP_ROWS // MOBA_BLOCK, HEADS, HEAD_DIM), lambda b, i: (b, i, 0, 0))],
        out_shape=[heads_shape, jax.ShapeDtypeStruct((bsz, HEADS, VT_ROWS, t), BF16), heads_shape, heads_shape,
                   jax.ShapeDtypeStruct((bsz, t // MOBA_BLOCK, HEADS, HEAD_DIM), F32)],
        compiler_params=pltpu.CompilerParams(
            dimension_semantics=("arbitrary", "arbitrary"), vmem_limit_bytes=VMEM_LIMIT),
        name="moba_proj",
    )(h, mod.reshape(bsz, 1, 3 * d), kv_g.reshape(1, d), pre_g.reshape(1, d), w_k, w_vt, w_in, cosf, sinf)


def _moba_scores(h, rows, nb, q_ref, k_ref, km_ref, s_ref, causal, blk):
    npast = nb - 1
    groups = (MOBA_BLOCK // SUBLANES, SUBLANES, TILE)
    q = q_ref[0, h, rows, :]
    keys = k_ref[0, h, 0:nb * MOBA_BLOCK, :]
    bias = None
    off = 0
    if npast > MOBA_TOPK:
        km = km_ref[0, :, h, :]
        km_hi = km.astype(BF16).astype(F32)
        keys = jnp.concatenate([jnp.concatenate([km_hi, km - km_hi], axis=0).astype(BF16), keys], axis=0)
        off = 2 * km.shape[0]
    s_all = _nt(keys, q)
    if off:
        g = s_all[0:off // 2] + s_all[off // 2:off]
        g = jnp.where(blk < npast, g, -jnp.inf)
        rank = jnp.zeros(g.shape, jnp.int32)
        for m in range(npast):
            gm = g[m:m + 1, :]
            rank = rank + ((gm > g) | ((gm == g) & (m < blk))).astype(jnp.int32)
        bias = jnp.where(rank < MOBA_TOPK, 0.0, NEG)

    m8 = None
    for n in range(nb):
        s = s_all[off + n * MOBA_BLOCK:off + (n + 1) * MOBA_BLOCK]
        if n == npast:
            s = jnp.where(causal, s, NEG)
        elif bias is not None:
            s = s + bias[n:n + 1, :]
        s_ref[n] = s
        mb = jnp.max(s.reshape(groups), axis=0)
        m8 = mb if m8 is None else jnp.maximum(m8, mb)
    return jnp.max(m8, axis=0, keepdims=True)


def _moba_values(h, rows, nb, m, vt_ref, z_ref, s_ref, og_ref):
    p = jnp.concatenate([jnp.exp2(s_ref[n] - m).astype(BF16) for n in range(nb)], axis=0)
    acc = _dot(vt_ref[0, h, :, 0:nb * MOBA_BLOCK], p)
    o = (acc[0:HEAD_DIM] * (1.0 / acc[HEAD_DIM:HEAD_DIM + 1])).T
    og_ref[h, rows, :] = (o * _silu(z_ref[0, h, rows, :].astype(F32))).astype(BF16)


def _moba_kernel(q_ref, k_ref, vt_ref, km_ref, z_ref, h_ref, mod_ref, wout_ref, pong_ref,
                 o_ref, *scratch):
    bufs, og_ref = scratch[:-1], scratch[-1]
    d = h_ref.shape[-1]
    i = pl.program_id(1)
    nblk = km_ref.shape[1]
    tiles = h_ref.shape[1] // TILE
    key = lax.broadcasted_iota(jnp.int32, (MOBA_BLOCK, TILE), 0)
    qry = lax.broadcasted_iota(jnp.int32, (MOBA_BLOCK, TILE), 1)
    causal = key <= qry
    blk = lax.broadcasted_iota(jnp.int32, (nblk, TILE), 0)

    for step in range(nblk // tiles):
        @pl.when(i == step)
        def _(step=step):
            units = [(t, h) for t in range(tiles) for h in range(HEADS)]

            def scores(n):
                t, h = units[n]
                return _moba_scores(h, slice(t * TILE, (t + 1) * TILE), step * tiles + t + 1, q_ref, k_ref, km_ref,
                                    bufs[n % len(bufs)], causal, blk)

            def values(n, m):
                t, h = units[n]
                _moba_values(h, slice(t * TILE, (t + 1) * TILE), step * tiles + t + 1, m, vt_ref, z_ref,
                             bufs[n % len(bufs)], og_ref)

            maxes = {n: scores(n) for n in range(LOOKAHEAD)}
            for n in range(len(units)):
                if n + LOOKAHEAD < len(units):
                    maxes[n + LOOKAHEAD] = scores(n + LOOKAHEAD)
                values(n, maxes.pop(n))

    y = _dot(jnp.concatenate([og_ref[h] for h in range(HEADS)], axis=1), wout_ref[...])
    gate = mod_ref[0][:, 2 * d:]
    o_ref[0] = h_ref[0] + gate * (_rms(y) * pong_ref[...])


def _moba(q, k, vt, kmean, z, h, mod, w_out, post_g):
    bsz, t, d = h.shape
    nt = t // TILE
    const = lambda *shape: pl.BlockSpec(shape, lambda b, i: (0,) * len(shape))
    tile = pl.BlockSpec((1, STEP_ROWS, d), lambda b, i: (b, i, 0))
    return pl.pallas_call(
        _moba_kernel,
        grid=(bsz, t // STEP_ROWS),
        in_specs=[pl.BlockSpec((1, HEADS, STEP_ROWS, HEAD_DIM), lambda b, i: (b, 0, i, 0)),
                  pl.BlockSpec((1, HEADS, t, HEAD_DIM), lambda b, i: (b, 0, 0, 0)),
                  pl.BlockSpec((1, HEADS, VT_ROWS, t), lambda b, i: (b, 0, 0, 0)),
                  pl.BlockSpec((1, nt, HEADS, HEAD_DIM), lambda b, i: (b, 0, 0, 0)),
                  pl.BlockSpec((1, HEADS, STEP_ROWS, HEAD_DIM), lambda b, i: (b, 0, i, 0)),
                  tile, pl.BlockSpec((1, 1, 3 * d), lambda b, i: (b, 0, 0)),
                  const(d, d), const(1, d)],
        out_specs=tile,
        out_shape=jax.ShapeDtypeStruct((bsz, t, d), F32),
        scratch_shapes=[
            *[pltpu.VMEM((nt, MOBA_BLOCK, TILE), F32)] * (LOOKAHEAD + 1),
            pltpu.VMEM((HEADS, STEP_ROWS, HEAD_DIM), BF16),
        ],
        compiler_params=pltpu.CompilerParams(
            dimension_semantics=("arbitrary", "arbitrary"), vmem_limit_bytes=VMEM_LIMIT),
        name="moba_attention",
    )(q, k, vt, kmean, z, h, mod.reshape(bsz, 1, 3 * d), w_out, post_g.reshape(1, d))


def kernel(x, c, positions, mod_w, mod_b, pre_norm_g, post_norm_g, a_w_in, a_w_out, a_out_norm_g,
           a_lb_logits, kv_norm_g, w_kv, b_w_in, b_w_out):
    assert x.shape[1] % STEP_ROWS == 0 and STEP_ROWS % TILE == 0 and x.shape[2] == HEADS * HEAD_DIM
    assert mod_w.shape[0] == 2 and a_w_in.shape[0] == 1 and b_w_in.shape[0] == 1
    mod = _modulation(c, mod_w, mod_b)
    cosf, sinf = _rope_tables(positions)
    h1 = _layer0(x, mod[0], pre_norm_g[0], a_w_in[0].astype(BF16), a_w_out[0].astype(BF16),
                 a_out_norm_g[0], a_lb_logits, post_norm_g[0])
    d = x.shape[2]
    k, vt, q, z, kmean = _proj1(h1, mod[1], kv_norm_g, pre_norm_g[1], w_kv[:, :d].astype(BF16),
                                w_kv[:, d:].astype(BF16).T, b_w_in[0].astype(BF16), cosf, sinf)
    return _moba(q, k, vt, kmean, z, h1, mod[1], b_w_out[0].astype(BF16), post_norm_g[1])
```
